```python
import functools
import jax, jax.numpy as jnp
from jax import lax
import numpy as np

D_MODEL = 2048
BATCH = 1
SEQ = 8192
DEPTH = 2
DEC_BATCH = 128
DEC_SEQ = 1
PAST_LEN = 2048
PAGE_SIZE = 128

H_FOX = 8
DH_FOX = 128
C_FOX = H_FOX * DH_FOX
H_RWKV = 16
DH_RWKV = 64
C_RWKV = H_RWKV * DH_RWKV
W_LORA = 64
A_LORA = 64
G_LORA = 160
D_FF = 5632
D_PLE = 256
Q_BLOCK = 128
NORM_EPS = 1e-6
LN_X_EPS = 64e-5

N_RWKV = 3 * C_RWKV + W_LORA + A_LORA + G_LORA
OFF_Q = 0
OFF_K = C_FOX
OFF_V = 2 * C_FOX
OFF_F = 3 * C_FOX
OFF_RWKV = OFF_F + H_FOX
OFF_GA = OFF_RWKV + N_RWKV
OFF_GB = OFF_GA + D_MODEL
N_IN = OFF_GB + D_MODEL
R_R = 0
R_K = C_RWKV
R_V = 2 * C_RWKV
R_WL = 3 * C_RWKV
R_AL = R_WL + W_LORA
R_GL = R_AL + A_LORA

kernel_name = 'fox_rwkv7_macaron_hybrid_step'


def rms_norm(x, g):
    xf = x.astype(jnp.float32)
    y = xf * lax.rsqrt(jnp.mean(xf * xf, axis=-1, keepdims=True) + NORM_EPS)
    return (y * g.astype(jnp.float32)).astype(x.dtype)


def swiglu(u, w_gate, w_up, w_down):
    return (jax.nn.silu(u @ w_gate) * (u @ w_up)) @ w_down


def fox_prompt(q, k, v, logf):
    b, s, h, dh = q.shape
    nb = s // Q_BLOCK
    c = jnp.cumsum(logf, axis=1)
    c_keys = c.transpose(0, 2, 1)
    q_blocks = q.reshape(b, nb, Q_BLOCK, h, dh).transpose(1, 0, 2, 3, 4)
    c_blocks = c.reshape(b, nb, Q_BLOCK, h).transpose(1, 0, 3, 2)
    k_pos = jnp.arange(s)
    scale = dh ** -0.5

    def attend_block(args):
        q_blk, c_blk, blk = args
        q_pos = blk * Q_BLOCK + jnp.arange(Q_BLOCK)
        logits = jnp.einsum('bqhd,bkhd->bhqk', q_blk, k).astype(jnp.float32) * scale
        logits = logits + (c_blk[..., :, None] - c_keys[:, :, None, :])
        logits = jnp.where(k_pos[None, :] <= q_pos[:, None], logits, -jnp.inf)
        probs = jax.nn.softmax(logits, axis=-1)
        return jnp.einsum('bhqk,bkhd->bqhd', probs.astype(v.dtype), v)

    out = lax.map(attend_block, (q_blocks, c_blocks, jnp.arange(nb)))
    return out.transpose(1, 0, 2, 3, 4).reshape(b, s, h, dh)


def fox_sample(q, k, v, logf, k_past, v_past, logf_past):
    b, t, h, dh = q.shape
    p_len = k_past.shape[1]
    c = jnp.cumsum(jnp.concatenate([logf_past.astype(jnp.float32), logf], axis=1), axis=1)
    c_keys = c.transpose(0, 2, 1)
    c_q = c_keys[:, :, p_len:]
    scale = dh ** -0.5
    logits = jnp.concatenate([
        jnp.einsum('bqhd,bkhd->bhqk', q, k_past).astype(jnp.float32),
        jnp.einsum('bqhd,bkhd->bhqk', q, k).astype(jnp.float32)], axis=-1) * scale
    logits = logits + (c_q[..., :, None] - c_keys[:, :, None, :])
    k_pos = jnp.arange(p_len + t)
    q_pos = p_len + jnp.arange(t)
    logits = jnp.where(k_pos[None, :] <= q_pos[:, None], logits, -jnp.inf)
    probs = jax.nn.softmax(logits, axis=-1)
    out = jnp.einsum('bhqk,bkhd->bqhd', probs[..., :p_len].astype(v.dtype), v_past)
    return out + jnp.einsum('bhqk,bkhd->bqhd', probs[..., p_len:].astype(v.dtype), v)


def rwkv7_mix(proj, shift_prev, wkv_prev, mu, w0, w2, a0, a2, g2, k_k, k_a, r_k, ln_g, ln_b):
    b, t, _ = proj.shape
    f32 = jnp.float32
    pf = proj.astype(f32)
    prev = jnp.concatenate([shift_prev.astype(f32)[:, None, :], pf[:, :-1]], axis=1)
    xm = pf + (prev - pf) * mu.astype(f32)
    r = xm[..., R_R:R_R + C_RWKV]
    k = xm[..., R_K:R_K + C_RWKV]
    v = xm[..., R_V:R_V + C_RWKV]
    wl = xm[..., R_WL:R_WL + W_LORA]
    al = xm[..., R_AL:R_AL + A_LORA]
    gl = xm[..., R_GL:R_GL + G_LORA]
    w_log = -jax.nn.softplus(-(w0.astype(f32) + jnp.tanh(wl) @ w2.astype(f32))) - 0.5
    decay = jnp.exp(-jnp.exp(w_log))
    a = jax.nn.sigmoid(a0.astype(f32) + al @ a2.astype(f32))
    g = jax.nn.sigmoid(gl) @ g2.astype(f32)
    heads = lambda z: z.reshape(b, t, H_RWKV, DH_RWKV)
    kk = heads(k * k_k.astype(f32))
    kk = kk / jnp.maximum(jnp.linalg.norm(kk, axis=-1, keepdims=True), 1e-12)
    k = k * (1.0 + (a - 1.0) * k_a.astype(f32))
    r, k, v, decay, a = heads(r), heads(k), heads(v), heads(decay), heads(a)
    a_vec = -kk
    b_vec = kk * a

    def step(state, inp):
        r_t, w_t, k_t, v_t, a_t, b_t = inp
        sa = jnp.einsum('bhvk,bhk->bhv', state, a_t)
        state = (state * w_t[:, :, None, :] + sa[..., None] * b_t[:, :, None, :]
                 + v_t[..., None] * k_t[:, :, None, :])
        return state, jnp.einsum('bhvk,bhk->bhv', state, r_t)

    seq_in = tuple(z.transpose(1, 0, 2, 3) for z in (r, decay, k, v, a_vec, b_vec))
    state_new, ys = lax.scan(step, wkv_prev.astype(f32), seq_in)
    y = ys.transpose(1, 0, 2, 3)
    mean = jnp.mean(y, axis=-1, keepdims=True)
    var = jnp.mean(jnp.square(y - mean), axis=-1, keepdims=True)
    y = ((y - mean) * lax.rsqrt(var + LN_X_EPS)).reshape(b, t, C_RWKV)
    y = y * ln_g.astype(f32) + ln_b.astype(f32)
    bonus = jnp.sum(r * k * r_k.astype(f32), axis=-1, keepdims=True) * v
    y = (y + bonus.reshape(b, t, C_RWKV)) * g
    return y, proj[:, -1], state_new


def trunk_layer(x, p, lp, attend, shift_prev, wkv_prev):
    x = x + 0.5 * swiglu(rms_norm(x, lp['ffn1_norm']), lp['ffn1_w_gate'], lp['ffn1_w_up'], lp['ffn1_w_down'])
    h = rms_norm(x, lp['mix_norm'])
    proj = h @ lp['w_in']
    b, t, _ = proj.shape
    q = proj[..., OFF_Q:OFF_Q + C_FOX].reshape(b, t, H_FOX, DH_FOX)
    k = proj[..., OFF_K:OFF_K + C_FOX].reshape(b, t, H_FOX, DH_FOX)
    v = proj[..., OFF_V:OFF_V + C_FOX].reshape(b, t, H_FOX, DH_FOX)
    logf = jax.nn.log_sigmoid(proj[..., OFF_F:OFF_F + H_FOX].astype(jnp.float32)
                              + lp['fox_b_f'].astype(jnp.float32))
    o_fox = attend(q, k, v, logf).reshape(b, t, C_FOX)
    o_rwkv, shift_new, wkv_new = rwkv7_mix(
        proj[..., OFF_RWKV:OFF_RWKV + N_RWKV], shift_prev, wkv_prev,
        lp['rwkv_mu'], lp['rwkv_w0'], lp['rwkv_w2'], lp['rwkv_a0'], lp['rwkv_a2'], lp['rwkv_g2'],
        lp['rwkv_k_k'], lp['rwkv_k_a'], lp['rwkv_r_k'], lp['rwkv_ln_g'], lp['rwkv_ln_b'])
    gate_fox = jax.nn.sigmoid(proj[..., OFF_GA:OFF_GA + D_MODEL])
    gate_rwkv = jax.nn.sigmoid(proj[..., OFF_GB:OFF_GB + D_MODEL])
    merged = (gate_fox * (o_fox @ lp['w_o_fox'])
              + gate_rwkv * (o_rwkv.astype(x.dtype) @ lp['w_o_rwkv']))
    x = x + merged @ lp['w_out']
    x = x + 0.5 * swiglu(rms_norm(x, lp['ffn2_norm']), lp['ffn2_w_gate'], lp['ffn2_w_up'], lp['ffn2_w_down'])
    u = rms_norm(x, lp['ple_norm'])
    x = x + jax.nn.sigmoid(u @ lp['ple_w_gate']) * (p @ lp['ple_w_proj'])
    return x, k, v, logf, wkv_new, shift_new


def setup_inputs(seed: int = 0) -> dict:
    key = jax.random.key(seed)
    ks = iter(jax.random.split(key, 64))
    f32 = jnp.float32

    def nrm(shape, scale=1.0):
        return jax.random.normal(next(ks), shape, f32) * scale

    def gain(shape):
        return 1.0 + nrm(shape, 0.05)

    n_pages = PAST_LEN // PAGE_SIZE
    n_used = DEC_BATCH * n_pages
    n_pool = n_used + max(1, n_used // 4)
    inp = {}
    inp['x_prompt'] = nrm((BATCH, SEQ, D_MODEL))
    inp['x_sample'] = nrm((DEC_BATCH, DEC_SEQ, D_MODEL))
    inp['cache_k'] = nrm((DEPTH, n_pool, PAGE_SIZE, H_FOX, DH_FOX))
    inp['cache_v'] = nrm((DEPTH, n_pool, PAGE_SIZE, H_FOX, DH_FOX))
    inp['cache_logf'] = jax.nn.log_sigmoid(2.0 + nrm((DEPTH, n_pool, PAGE_SIZE, H_FOX)))
    inp['state_wkv'] = nrm((DEPTH, DEC_BATCH, H_RWKV, DH_RWKV, DH_RWKV), 0.3)
    inp['state_shift'] = nrm((DEPTH, DEC_BATCH, N_RWKV))
    inp['page_table'] = jax.random.permutation(next(ks), n_pool)[:n_used].reshape(DEC_BATCH, n_pages).astype(jnp.int32)
    inp['p_prompt'] = nrm((DEPTH, BATCH, SEQ, D_PLE))
    inp['p_sample'] = nrm((DEPTH, DEC_BATCH, DEC_SEQ, D_PLE))
    inp['ffn1_norm'] = gain((DEPTH, D_MODEL))
    inp['ffn1_w_gate'] = nrm((DEPTH, D_MODEL, D_FF), D_MODEL ** -0.5)
    inp['ffn1_w_up'] = nrm((DEPTH, D_MODEL, D_FF), D_MODEL ** -0.5)
    inp['ffn1_w_down'] = nrm((DEPTH, D_FF, D_MODEL), D_FF ** -0.5)
    inp['mix_norm'] = gain((DEPTH, D_MODEL))
    inp['w_in'] = nrm((DEPTH, D_MODEL, N_IN), D_MODEL ** -0.5)
    inp['fox_b_f'] = 2.0 + nrm((DEPTH, H_FOX), 0.1)
    inp['rwkv_mu'] = jax.random.uniform(next(ks), (DEPTH, N_RWKV), f32)
    inp['rwkv_w0'] = nrm((DEPTH, C_RWKV), 0.5)
    inp['rwkv_w2'] = nrm((DEPTH, W_LORA, C_RWKV), 0.5 * W_LORA ** -0.5)
    inp['rwkv_a0'] = nrm((DEPTH, C_RWKV), 0.1)
    inp['rwkv_a2'] = nrm((DEPTH, A_LORA, C_RWKV), 0.5 * A_LORA ** -0.5)
    inp['rwkv_g2'] = nrm((DEPTH, G_LORA, C_RWKV), G_LORA ** -0.5)
    inp['rwkv_k_k'] = 0.85 + nrm((DEPTH, C_RWKV), 0.05)
    inp['rwkv_k_a'] = gain((DEPTH, C_RWKV))
    inp['rwkv_r_k'] = nrm((DEPTH, H_RWKV, DH_RWKV), 0.1)
    inp['rwkv_ln_g'] = gain((DEPTH, C_RWKV))
    inp['rwkv_ln_b'] = nrm((DEPTH, C_RWKV), 0.02)
    inp['w_o_fox'] = nrm((DEPTH, C_FOX, D_MODEL), C_FOX ** -0.5)
    inp['w_o_rwkv'] = nrm((DEPTH, C_RWKV, D_MODEL), C_RWKV ** -0.5)
    inp['w_out'] = nrm((DEPTH, D_MODEL, D_MODEL), D_MODEL ** -0.5)
    inp['ffn2_norm'] = gain((DEPTH, D_MODEL))
    inp['ffn2_w_gate'] = nrm((DEPTH, D_MODEL, D_FF), D_MODEL ** -0.5)
    inp['ffn2_w_up'] = nrm((DEPTH, D_MODEL, D_FF), D_MODEL ** -0.5)
    inp['ffn2_w_down'] = nrm((DEPTH, D_FF, D_MODEL), D_FF ** -0.5)
    inp['ple_norm'] = gain((DEPTH, D_MODEL))
    inp['ple_w_gate'] = nrm((DEPTH, D_MODEL, D_MODEL), D_MODEL ** -0.5)
    inp['ple_w_proj'] = nrm((DEPTH, D_PLE, D_MODEL), D_PLE ** -0.5)
    inp['final_norm'] = gain((D_MODEL,))
    return inp


def reference(x_prompt, x_sample, cache_k, cache_v, cache_logf, state_wkv, state_shift, page_table,
              p_prompt, p_sample,
              ffn1_norm, ffn1_w_gate, ffn1_w_up, ffn1_w_down, mix_norm, w_in, fox_b_f,
              rwkv_mu, rwkv_w0, rwkv_w2, rwkv_a0, rwkv_a2, rwkv_g2, rwkv_k_k, rwkv_k_a, rwkv_r_k,
              rwkv_ln_g, rwkv_ln_b, w_o_fox, w_o_rwkv, w_out,
              ffn2_norm, ffn2_w_gate, ffn2_w_up, ffn2_w_down, ple_norm, ple_w_gate, ple_w_proj,
              final_norm):
    n_dec = page_table.shape[0]
    p_len = page_table.shape[1] * cache_k.shape[2]
    n_pr = x_prompt.shape[0]
    xp, xs = x_prompt, x_sample
    kp_l, vp_l, lfp_l, wkvp_l, shp_l = [], [], [], [], []
    ks_l, vs_l, lfs_l, wkvs_l, shs_l = [], [], [], [], []
    for i in range(DEPTH):
        lp = dict(
            ffn1_norm=ffn1_norm[i], ffn1_w_gate=ffn1_w_gate[i], ffn1_w_up=ffn1_w_up[i], ffn1_w_down=ffn1_w_down[i],
            mix_norm=mix_norm[i], w_in=w_in[i], fox_b_f=fox_b_f[i],
            rwkv_mu=rwkv_mu[i], rwkv_w0=rwkv_w0[i], rwkv_w2=rwkv_w2[i], rwkv_a0=rwkv_a0[i], rwkv_a2=rwkv_a2[i],
            rwkv_g2=rwkv_g2[i], rwkv_k_k=rwkv_k_k[i], rwkv_k_a=rwkv_k_a[i], rwkv_r_k=rwkv_r_k[i],
            rwkv_ln_g=rwkv_ln_g[i], rwkv_ln_b=rwkv_ln_b[i],
            w_o_fox=w_o_fox[i], w_o_rwkv=w_o_rwkv[i], w_out=w_out[i],
            ffn2_norm=ffn2_norm[i], ffn2_w_gate=ffn2_w_gate[i], ffn2_w_up=ffn2_w_up[i], ffn2_w_down=ffn2_w_down[i],
            ple_norm=ple_norm[i], ple_w_gate=ple_w_gate[i], ple_w_proj=ple_w_proj[i])
        xp, kp, vp, lfp, wkvp, shp = trunk_layer(
            xp, p_prompt[i], lp, fox_prompt,
            jnp.zeros((n_pr, N_RWKV), xp.dtype),
            jnp.zeros((n_pr, H_RWKV, DH_RWKV, DH_RWKV), jnp.float32))
        kp_l.append(kp); vp_l.append(vp); lfp_l.append(lfp); wkvp_l.append(wkvp); shp_l.append(shp)
        k_past = cache_k[i][page_table].reshape(n_dec, p_len, H_FOX, DH_FOX)
        v_past = cache_v[i][page_table].reshape(n_dec, p_len, H_FOX, DH_FOX)
        lf_past = cache_logf[i][page_table].reshape(n_dec, p_len, H_FOX)
        attend = functools.partial(fox_sample, k_past=k_past, v_past=v_past, logf_past=lf_past)
        xs, kn, vn, lfn, wkvn, shn = trunk_layer(xs, p_sample[i], lp, attend, state_shift[i], state_wkv[i])
        ks_l.append(kn); vs_l.append(vn); lfs_l.append(lfn); wkvs_l.append(wkvn); shs_l.append(shn)
    y_prompt = rms_norm(xp, final_norm)
    y_sample = rms_norm(xs, final_norm)
    k_prompt = jnp.stack(kp_l)
    v_prompt = jnp.stack(vp_l)
    logf_prompt = jnp.stack(lfp_l)
    wkv_prompt = jnp.stack(wkvp_l)
    shift_prompt = jnp.stack(shp_l)
    k_sample = jnp.stack(ks_l)
    v_sample = jnp.stack(vs_l)
    logf_sample = jnp.stack(lfs_l)
    wkv_sample = jnp.stack(wkvs_l)
    shift_sample = jnp.stack(shs_l)
    return (y_prompt, y_sample, k_prompt, v_prompt, logf_prompt, wkv_prompt, shift_prompt,
            k_sample, v_sample, logf_sample, wkv_sample, shift_sample)
```

```python
import functools

import jax
import jax.numpy as jnp
from jax import lax
from jax.experimental import pallas as pl
from jax.experimental.pallas import tpu as pltpu

F32 = jnp.float32
BF16 = jnp.bfloat16

NORM_EPS = 1e-6
LN_X_EPS = 64e-5
KK_EPS = 1e-12

LANES = 128
VMEM_LIMIT = 56 * 1024 * 1024

H_FOX = 8
DH_FOX = 128
C_FOX = H_FOX * DH_FOX
H_RWKV = 16
DH_RWKV = 64
C_RWKV = H_RWKV * DH_RWKV
W_LORA = 64
A_LORA = 64
G_LORA = 160
N_RWKV = 3 * C_RWKV + W_LORA + A_LORA + G_LORA
RP_WL = 3 * C_RWKV
RP_AL = RP_WL + LANES
RP_GL = RP_AL + LANES
RP_W = RP_GL + 2 * LANES
SCAN_PASSES = 3
SCAN_CHUNK = 64


def _cparams(sem):
    return pltpu.CompilerParams(dimension_semantics=sem, vmem_limit_bytes=VMEM_LIMIT)


def _sigmoid(x):
    return 1.0 / (1.0 + jnp.exp(-x))


def _softplus(x):
    return jnp.maximum(x, 0.0) + jnp.log(1.0 + jnp.exp(-jnp.abs(x)))


def _rms_rows(x, g):
    ms = jnp.mean(x * x, axis=-1, keepdims=True)
    return x * lax.rsqrt(ms + NORM_EPS) * g


def _dot(a, b):
    return jnp.dot(a, b, preferred_element_type=F32)


def _split2(x):
    hi = x.astype(BF16)
    lo = (x - hi.astype(F32)).astype(BF16)
    return hi, lo


def _ffn_kernel(x_ref, g_ref, wg_ref, wu_ref, wd_ref, o_ref, h_scr, acc_scr):
    f = pl.program_id(1)

    @pl.when(f == 0)
    def _():
        h_scr[...] = _rms_rows(x_ref[...], g_ref[...]).astype(BF16)
        acc_scr[...] = jnp.zeros_like(acc_scr)

    h = h_scr[...]
    gate = _dot(h, wg_ref[...])
    up = _dot(h, wu_ref[...])
    act = (gate * _sigmoid(gate) * up).astype(BF16)
    acc_scr[...] += _dot(act, wd_ref[...])

    @pl.when(f == pl.num_programs(1) - 1)
    def _():
        o_ref[...] = x_ref[...] + 0.5 * acc_scr[...]


def _ffn(x, g, wg, wu, wd, *, tm, tf):
    m, d = x.shape
    dff = wg.shape[1]
    return pl.pallas_call(
        _ffn_kernel,
        grid=(m // tm, dff // tf),
        in_specs=[
            pl.BlockSpec((tm, d), lambda i, f: (i, 0)),
            pl.BlockSpec((1, d), lambda i, f: (0, 0)),
            pl.BlockSpec((d, tf), lambda i, f: (0, f)),
            pl.BlockSpec((d, tf), lambda i, f: (0, f)),
            pl.BlockSpec((tf, d), lambda i, f: (f, 0)),
        ],
        out_specs=pl.BlockSpec((tm, d), lambda i, f: (i, 0)),
        out_shape=jax.ShapeDtypeStruct((m, d), F32),
        scratch_shapes=[pltpu.VMEM((tm, d), BF16), pltpu.VMEM((tm, d), F32)],
        compiler_params=_cparams(("parallel", "arbitrary")),
        name="ffn",
    )(x, g, wg, wu, wd)


def _proj_kernel(x_ref, g_ref, w_ref, aux_ref, *rest, mode):
    outs, h_scr = rest[:-1], rest[-1]

    @pl.when(pl.program_id(1) == 0)
    def _():
        h_scr[...] = _rms_rows(x_ref[...], g_ref[...]).astype(BF16)

    acc = _dot(h_scr[...], w_ref[...])
    if mode == "qkv":
        outs[0][...] = acc
        outs[1][...] = (acc * aux_ref[...]).astype(BF16)
    elif mode == "logf":
        outs[0][...] = -_softplus(-(acc + aux_ref[...]))
    elif mode == "sigmoid":
        outs[0][...] = _sigmoid(acc)
    else:
        outs[0][...] = acc


def _proj(x, g, w, aux, *, mode, tm, tn):
    m, d = x.shape
    n = w.shape[1]
    out_shape = [jax.ShapeDtypeStruct((m, n), F32)]
    if mode == "qkv":
        out_shape.append(jax.ShapeDtypeStruct((m, n), BF16))
    o_spec = pl.BlockSpec((tm, tn), lambda i, j: (i, j))
    return pl.pallas_call(
        functools.partial(_proj_kernel, mode=mode),
        grid=(m // tm, n // tn),
        in_specs=[
            pl.BlockSpec((tm, d), lambda i, j: (i, 0)),
            pl.BlockSpec((1, d), lambda i, j: (0, 0)),
            pl.BlockSpec((d, tn), lambda i, j: (0, j)),
            pl.BlockSpec((1, tn), lambda i, j: (0, j)),
        ],
        out_specs=[o_spec] * len(out_shape),
        out_shape=out_shape,
        scratch_shapes=[pltpu.VMEM((tm, d), BF16)],
        compiler_params=_cparams(("parallel", "arbitrary")),
        name="proj_" + mode,
    )(x, g, w, aux)


def _merge_kernel(x_ref, of_ref, or_ref, gf_ref, gr_ref, wof_ref, wor_ref, wout_ref, o_ref, acc_scr):
    n = pl.program_id(1)

    @pl.when(n == 0)
    def _():
        acc_scr[...] = jnp.zeros_like(acc_scr)

    merged = gf_ref[...] * _dot(of_ref[...], wof_ref[...]) + gr_ref[...] * _dot(or_ref[...], wor_ref[...])
    acc_scr[...] += _dot(merged.astype(BF16), wout_ref[...])

    @pl.when(n == pl.num_programs(1) - 1)
    def _():
        o_ref[...] = x_ref[...] + acc_scr[...]


def _merge(x, o_fox, o_rwkv, gates, wof, wor, wout, *, tm, tn):
    m, d = x.shape
    nb = d // tn
    return pl.pallas_call(
        _merge_kernel,
        grid=(m // tm, nb),
        in_specs=[
            pl.BlockSpec((tm, d), lambda i, n: (i, 0)),
            pl.BlockSpec((tm, o_fox.shape[1]), lambda i, n: (i, 0)),
            pl.BlockSpec((tm, o_rwkv.shape[1]), lambda i, n: (i, 0)),
            pl.BlockSpec((tm, tn), lambda i, n: (i, n)),
            pl.BlockSpec((tm, tn), lambda i, n: (i, n + nb)),
            pl.BlockSpec((wof.shape[0], tn), lambda i, n: (0, n)),
            pl.BlockSpec((wor.shape[0], tn), lambda i, n: (0, n)),
            pl.BlockSpec((tn, d), lambda i, n: (n, 0)),
        ],
        out_specs=pl.BlockSpec((tm, d), lambda i, n: (i, 0)),
        out_shape=jax.ShapeDtypeStruct((m, d), F32),
        scratch_shapes=[pltpu.VMEM((tm, d), F32)],
        compiler_params=_cparams(("parallel", "arbitrary")),
        name="merge",
    )(x, o_fox, o_rwkv, gates, gates, wof, wor, wout)


def _ple_kernel(x_ref, xc_ref, g_ref, p_ref, wg_ref, wp_ref, o_ref, u_scr):
    @pl.when(pl.program_id(1) == 0)
    def _():
        u_scr[...] = _rms_rows(x_ref[...], g_ref[...]).astype(BF16)

    gate = _sigmoid(_dot(u_scr[...], wg_ref[...]))
    emb = _dot(p_ref[...].astype(BF16), wp_ref[...])
    o_ref[...] = xc_ref[...] + gate * emb


def _ple(x, g, p, wg, wp, *, tm, tn):
    m, d = x.shape
    return pl.pallas_call(
        _ple_kernel,
        grid=(m // tm, d // tn),
        in_specs=[
            pl.BlockSpec((tm, d), lambda i, j: (i, 0)),
            pl.BlockSpec((tm, tn), lambda i, j: (i, j)),
            pl.BlockSpec((1, d), lambda i, j: (0, 0)),
            pl.BlockSpec((tm, p.shape[1]), lambda i, j: (i, 0)),
            pl.BlockSpec((d, tn), lambda i, j: (0, j)),
            pl.BlockSpec((p.shape[1], tn), lambda i, j: (0, j)),
        ],
        out_specs=pl.BlockSpec((tm, tn), lambda i, j: (i, j)),
        out_shape=jax.ShapeDtypeStruct((m, d), F32),
        scratch_shapes=[pltpu.VMEM((tm, d), BF16)],
        compiler_params=_cparams(("parallel", "arbitrary")),
        name="ple",
    )(x, x, g, p, wg, wp)


def _norm_kernel(x_ref, g_ref, o_ref):
    o_ref[...] = _rms_rows(x_ref[...], g_ref[...])


def _final_norm(x, g, *, tm):
    m, d = x.shape
    return pl.pallas_call(
        _norm_kernel,
        grid=(m // tm,),
        in_specs=[pl.BlockSpec((tm, d), lambda i: (i, 0)), pl.BlockSpec((1, d), lambda i: (0, 0))],
        out_specs=pl.BlockSpec((tm, d), lambda i: (i, 0)),
        out_shape=jax.ShapeDtypeStruct((m, d), F32),
        compiler_params=_cparams(("parallel",)),
        name="final_norm",
    )(x, g)


def _cumsum_kernel(lf_ref, o_ref):
    h, s = lf_ref.shape
    row = lax.broadcasted_iota(jnp.int32, (LANES, LANES), 0)
    col = lax.broadcasted_iota(jnp.int32, (LANES, LANES), 1)
    tri = jnp.where(row <= col, 1.0, 0.0)

    def body(c, carry):
        off = pl.multiple_of(c * LANES, LANES)
        y = jnp.dot(lf_ref[:, pl.ds(off, LANES)], tri, precision=lax.Precision.HIGHEST,
                    preferred_element_type=F32) + carry
        o_ref[:, pl.ds(off, LANES)] = y
        return y[:, LANES - 1:LANES]

    lax.fori_loop(0, s // LANES, body, jnp.zeros((h, 1), F32))


def _cumsum_rows(lf_t):
    return pl.pallas_call(
        _cumsum_kernel,
        out_shape=jax.ShapeDtypeStruct(lf_t.shape, F32),
        compiler_params=pltpu.CompilerParams(vmem_limit_bytes=VMEM_LIMIT),
        name="logf_cumsum",
    )(lf_t)


def _fox_prompt_kernel(q_ref, k_ref, v_ref, cq_ref, ck_ref, o_ref, m_scr, l_scr, acc_scr, *, tq, tk):
    qi = pl.program_id(0)
    kj = pl.program_id(1)

    @pl.when(kj == 0)
    def _():
        m_scr[...] = jnp.full_like(m_scr, -jnp.inf)
        l_scr[...] = jnp.zeros_like(l_scr)
        acc_scr[...] = jnp.zeros_like(acc_scr)

    @pl.when(kj <= qi)
    def _():
        q_pos = qi * tq + lax.broadcasted_iota(jnp.int32, (tq, tk), 0)
        k_pos = kj * tk + lax.broadcasted_iota(jnp.int32, (tq, tk), 1)
        causal = k_pos <= q_pos
        for h in range(H_FOX):
            sl = slice(h * DH_FOX, (h + 1) * DH_FOX)
            s = lax.dot_general(q_ref[:, sl], k_ref[:, sl], (((1,), (1,)), ((), ())),
                                preferred_element_type=F32)
            s = s + (cq_ref[:, h:h + 1] - ck_ref[h:h + 1, :])
            s = jnp.where(causal, s, -jnp.inf)
            m_old = m_scr[h]
            m_new = jnp.maximum(m_old, jnp.max(s, axis=-1, keepdims=True))
            alpha = jnp.exp(m_old - m_new)
            p = jnp.exp(s - m_new)
            l_scr[h] = alpha * l_scr[h] + jnp.sum(p, axis=-1, keepdims=True)
            acc_scr[h] = alpha * acc_scr[h] + _dot(p.astype(BF16), v_ref[:, sl])
            m_scr[h] = m_new

    @pl.when(kj == qi)
    def _():
        for h in range(H_FOX):
            o_ref[:, h * DH_FOX:(h + 1) * DH_FOX] = (acc_scr[h] / l_scr[h]).astype(o_ref.dtype)


def _fox_prompt(qkv_bf, c_col, c_row, *, s_len, tq, tk):
    nq, nk = s_len // tq, s_len // tk
    return pl.pallas_call(
        functools.partial(_fox_prompt_kernel, tq=tq, tk=tk),
        grid=(nq, nk),
        in_specs=[
            pl.BlockSpec((tq, C_FOX), lambda i, j: (i, 0)),
            pl.BlockSpec((tk, C_FOX), lambda i, j: (jnp.minimum(i, j), 1)),
            pl.BlockSpec((tk, C_FOX), lambda i, j: (jnp.minimum(i, j), 2)),
            pl.BlockSpec((tq, H_FOX), lambda i, j: (i, 0)),
            pl.BlockSpec((H_FOX, tk), lambda i, j: (0, jnp.minimum(i, j))),
        ],
        out_specs=pl.BlockSpec((tq, C_FOX), lambda i, j: (i, 0)),
        out_shape=jax.ShapeDtypeStruct((s_len, C_FOX), BF16),
        scratch_shapes=[
            pltpu.VMEM((H_FOX, tq, 1), F32),
            pltpu.VMEM((H_FOX, tq, 1), F32),
            pltpu.VMEM((H_FOX, tq, DH_FOX), F32),
        ],
        compiler_params=_cparams(("parallel", "arbitrary")),
        name="fox_prompt",
    )(qkv_bf, qkv_bf, qkv_bf, c_col, c_row)


def _fox_decode_kernel(pt_ref, q_ref, kn_ref, vn_ref, lfn_ref, kc_ref, vc_ref, lfc_ref, o_ref,
                       m_scr, l_scr, acc_scr, carry_scr, *, scale):
    del pt_ref
    p = pl.program_id(1)
    page = kc_ref.shape[0]
    head_of_lane = lax.broadcasted_iota(jnp.int32, (H_FOX, C_FOX), 1) // DH_FOX
    head_row = lax.broadcasted_iota(jnp.int32, (H_FOX, C_FOX), 0)
    sel = head_of_lane == head_row
    sel_bf = jnp.where(sel, 1.0, 0.0).astype(BF16)
    q = q_ref[...] * scale

    @pl.when(p == 0)
    def _():
        m_scr[...] = jnp.full_like(m_scr, -jnp.inf)
        l_scr[...] = jnp.zeros_like(l_scr)
        acc_scr[...] = jnp.zeros_like(acc_scr)
        carry_scr[...] = lfn_ref[...]

    nt = (((1,), (1,)), ((), ()))
    kq_hi, kq_lo = _split2(kc_ref[...] * q)
    s = (lax.dot_general(sel_bf, kq_hi, nt, preferred_element_type=F32)
         + lax.dot_general(sel_bf, kq_lo, nt, preferred_element_type=F32))
    later = (lax.broadcasted_iota(jnp.int32, (page, page), 0)
             > lax.broadcasted_iota(jnp.int32, (page, page), 1))
    later = jnp.where(later, 1.0, 0.0)
    lf = lfc_ref[...]
    carry = carry_scr[...]
    s = s + jnp.dot(lf, later, precision=lax.Precision.HIGHEST, preferred_element_type=F32) + carry
    carry_scr[...] = carry + jnp.sum(lf, axis=-1, keepdims=True)

    m_old = m_scr[...]
    m_new = jnp.maximum(m_old, jnp.max(s, axis=-1, keepdims=True))
    alpha = jnp.exp(m_old - m_new)
    pr = jnp.exp(s - m_new)
    l_scr[...] = alpha * l_scr[...] + jnp.sum(pr, axis=-1, keepdims=True)
    acc_scr[...] = alpha * acc_scr[...] + _dot(pr.astype(BF16), vc_ref[...].astype(BF16))
    m_scr[...] = m_new

    @pl.when(p == pl.num_programs(1) - 1)
    def _():
        s_new = jnp.sum(jnp.where(sel, kn_ref[...] * q, 0.0), axis=-1, keepdims=True)
        m_old = m_scr[...]
        m_fin = jnp.maximum(m_old, s_new)
        alpha = jnp.exp(m_old - m_fin)
        p_new = jnp.exp(s_new - m_fin)
        l_fin = alpha * l_scr[...] + p_new
        out = (alpha * acc_scr[...] + p_new * vn_ref[...]) / l_fin
        o_ref[...] = jnp.sum(jnp.where(sel, out, 0.0), axis=0, keepdims=True).astype(o_ref.dtype)


def _fox_decode(page_table, q, k_new, v_new, lf_new, cache_k, cache_v, cache_lf_t):
    n, n_pages = page_table.shape
    page = cache_k.shape[1]
    pt_flat = page_table.reshape(-1)

    def page_map(b, p, pt):
        return (pt[b * n_pages + (n_pages - 1 - p)], 0, 0)

    row_spec = pl.BlockSpec((None, 1, C_FOX), lambda b, p, pt: (b, 0, 0))
    grid_spec = pltpu.PrefetchScalarGridSpec(
        num_scalar_prefetch=1,
        grid=(n, n_pages),
        in_specs=[
            row_spec, row_spec, row_spec,
            pl.BlockSpec((None, H_FOX, 1), lambda b, p, pt: (b, 0, 0)),
            pl.BlockSpec((None, page, C_FOX), page_map),
            pl.BlockSpec((None, page, C_FOX), page_map),
            pl.BlockSpec((None, H_FOX, page), page_map),
        ],
        out_specs=row_spec,
        scratch_shapes=[
            pltpu.VMEM((H_FOX, 1), F32),
            pltpu.VMEM((H_FOX, 1), F32),
            pltpu.VMEM((H_FOX, C_FOX), F32),
            pltpu.VMEM((H_FOX, 1), F32),
        ],
    )
    return pl.pallas_call(
        functools.partial(_fox_decode_kernel, scale=DH_FOX ** -0.5),
        grid_spec=grid_spec,
        out_shape=jax.ShapeDtypeStruct((n, 1, C_FOX), BF16),
        compiler_params=_cparams(("parallel", "arbitrary")),
        name="fox_decode",
    )(pt_flat, q, k_new, v_new, lf_new, cache_k, cache_v, cache_lf_t)


def _seg_sum(x, e_ref, et_ref):
    hi, lo = _split2(x)
    s = _dot(hi, e_ref[...]) + _dot(lo, e_ref[...])
    shi, slo = _split2(s)
    return _dot(shi, et_ref[...]) + _dot(slo, et_ref[...])


def _rwkv_prep_kernel(pr_ref, pv_ref, mu_ref, w0_ref, a0_ref, kk_ref, ka_ref, rk_ref,
                      w2_ref, a2_ref, g2_ref, e_ref, et_ref,
                      r_out, lw_out, k_out, v_out, a_out, b_out, g_out, bonus_out, *, sequence):
    pr = pr_ref[...]
    if sequence:
        first = jnp.where(pl.program_id(0) == 0, 0.0, pv_ref[7:8, :])
        prev = jnp.where(lax.broadcasted_iota(jnp.int32, pr.shape, 0) == 0, first,
                         pltpu.roll(pr, 1, axis=0))
    else:
        prev = pv_ref[...]
    xm = pr + (prev - pr) * mu_ref[...]
    r = xm[:, 0:C_RWKV]
    k = xm[:, C_RWKV:2 * C_RWKV]
    v = xm[:, 2 * C_RWKV:3 * C_RWKV]
    wl = xm[:, RP_WL:RP_AL]
    al = xm[:, RP_AL:RP_GL]
    gl = xm[:, RP_GL:RP_W]

    w_log = -_softplus(-(w0_ref[...] + _dot(jnp.tanh(wl).astype(BF16), w2_ref[...]))) - 0.5
    a_sig = _sigmoid(a0_ref[...] + _dot(al.astype(BF16), a2_ref[...]))
    g = _dot(_sigmoid(gl).astype(BF16), g2_ref[...])

    kk = k * kk_ref[...]
    norm = jnp.sqrt(_seg_sum(kk * kk, e_ref, et_ref))
    kk = kk / jnp.maximum(norm, KK_EPS)
    k_mod = k * (1.0 + (a_sig - 1.0) * ka_ref[...])

    r_out[...] = r
    lw_out[...] = -jnp.exp(w_log)
    k_out[...] = k_mod
    v_out[...] = v
    a_out[...] = -kk
    b_out[...] = kk * a_sig
    g_out[...] = g
    bonus_out[...] = _seg_sum(r * k_mod * rk_ref[...], e_ref, et_ref) * v


def _rwkv_prep(proj, prev, vecs, mats, sel, *, sequence, tm):
    m = proj.shape[0]
    mu, w0, a0, k_k, k_a, r_k = vecs
    w2, a2, g2 = mats
    e, et = sel
    row = lambda i: (i, 0)
    fix = lambda i: (0, 0)
    if sequence:
        prev_spec = pl.BlockSpec((8, RP_W), lambda i: (jnp.maximum(i * (tm // 8) - 1, 0), 0))
    else:
        prev_spec = pl.BlockSpec((tm, RP_W), row)
    vec_spec = pl.BlockSpec((1, C_RWKV), fix)
    out = jax.ShapeDtypeStruct((m, C_RWKV), F32)
    return pl.pallas_call(
        functools.partial(_rwkv_prep_kernel, sequence=sequence),
        grid=(m // tm,),
        in_specs=[
            pl.BlockSpec((tm, RP_W), row), prev_spec, pl.BlockSpec((1, RP_W), fix),
            vec_spec, vec_spec, vec_spec, vec_spec, vec_spec,
            pl.BlockSpec(w2.shape, fix), pl.BlockSpec(a2.shape, fix), pl.BlockSpec(g2.shape, fix),
            pl.BlockSpec(e.shape, fix), pl.BlockSpec(et.shape, fix),
        ],
        out_specs=[pl.BlockSpec((tm, C_RWKV), row)] * 8,
        out_shape=[out] * 8,
        compiler_params=_cparams(("parallel",)),
        name="rwkv_prep_seq" if sequence else "rwkv_prep_batch",
    )(proj, prev, mu, w0, a0, k_k, k_a, r_k, w2, a2, g2, e, et)


def _mm(a, b, dims, passes):
    dn = (dims, ((), ()))
    if passes == 1:
        return lax.dot_general(a.astype(BF16), b.astype(BF16), dn, preferred_element_type=F32)
    ah, al = _split2(a)
    bh, bl = _split2(b)
    return (lax.dot_general(ah, bh, dn, preferred_element_type=F32)
            + lax.dot_general(ah, bl, dn, preferred_element_type=F32)
            + lax.dot_general(al, bh, dn, preferred_element_type=F32))


_NN = ((1,), (0,))
_NT = ((1,), (1,))
_TN = ((0,), (0,))


def _rwkv_scan_kernel(r_ref, lw_ref, k_ref, v_ref, a_ref, b_ref, y_ref, s_out, s_scr, *, passes):
    c = pl.program_id(0)
    ch = r_ref.shape[0]
    mm = functools.partial(_mm, passes=passes)

    @pl.when(c == 0)
    def _():
        s_scr[...] = jnp.zeros_like(s_scr)

    row = lax.broadcasted_iota(jnp.int32, (ch, ch), 0)
    col = lax.broadcasted_iota(jnp.int32, (ch, ch), 1)
    incl = row >= col
    strict = row > col
    eye = jnp.where(row == col, 1.0, 0.0)

    lw = lw_ref[...]
    cs = jnp.dot(jnp.where(incl, 1.0, 0.0), lw, precision=lax.Precision.HIGHEST, preferred_element_type=F32)
    p_incl = jnp.exp(cs)
    p_inv = jnp.exp(-cs)
    p_excl = jnp.exp(cs - lw)
    r_t = r_ref[...] * p_incl
    a_t = a_ref[...] * p_excl
    b_t = b_ref[...] * p_inv
    k_t = k_ref[...] * p_inv
    v_all = v_ref[...]
    p_end = p_incl[ch - 1:ch, :]

    for h in range(H_RWKV):
        sl = slice(h * DH_RWKV, (h + 1) * DH_RWKV)
        am, rm, bm, km, vm = a_t[:, sl], r_t[:, sl], b_t[:, sl], k_t[:, sl], v_all[:, sl]
        s0 = s_scr[h]
        l_ab = jnp.where(strict, mm(am, bm, _NT), 0.0)
        l_ak = jnp.where(strict, mm(am, km, _NT), 0.0)
        m_rb = jnp.where(incl, mm(rm, bm, _NT), 0.0)
        m_rk = jnp.where(incl, mm(rm, km, _NT), 0.0)
        t_inv = eye + l_ab
        pw = l_ab
        span = 2
        while span < ch:
            pw = mm(pw, pw, _NN)
            t_inv = t_inv + mm(t_inv, pw, _NN)
            span *= 2
        u = mm(t_inv, mm(am, s0, _NT) + mm(l_ak, vm, _NN), _NN)
        y_ref[:, sl] = mm(rm, s0, _NT) + mm(m_rb, u, _NN) + mm(m_rk, vm, _NN)
        s_scr[h] = (s0 + mm(u, bm, _TN) + mm(vm, km, _TN)) * p_end[:, sl]

    @pl.when(c == pl.num_programs(0) - 1)
    def _():
        s_out[...] = s_scr[...]


def _rwkv_scan(r, lw, k, v, a, b, *, passes):
    t = r.shape[0]
    spec = pl.BlockSpec((SCAN_CHUNK, C_RWKV), lambda c: (c, 0))
    st = (H_RWKV, DH_RWKV, DH_RWKV)
    return pl.pallas_call(
        functools.partial(_rwkv_scan_kernel, passes=passes),
        grid=(t // SCAN_CHUNK,),
        in_specs=[spec] * 6,
        out_specs=[spec, pl.BlockSpec(st, lambda c: (0, 0, 0))],
        out_shape=[jax.ShapeDtypeStruct((t, C_RWKV), F32), jax.ShapeDtypeStruct(st, F32)],
        scratch_shapes=[pltpu.VMEM(st, F32)],
        compiler_params=_cparams(("arbitrary",)),
        name="rwkv_scan",
    )(r, lw, k, v, a, b)


def _rwkv_step_kernel(r_ref, lw_ref, k_ref, v_ref, a_ref, b_ref, s_ref, y_ref, s_out):
    s = s_ref[...]
    dv, dk = s.shape[-2:]
    eye = (lax.broadcasted_iota(jnp.int32, (dv, dk), 0) == lax.broadcasted_iota(jnp.int32, (dv, dk), 1))
    sa = jnp.sum(s * a_ref[...], axis=-1, keepdims=True)
    v_col = jnp.sum(jnp.where(eye, v_ref[...], 0.0), axis=-1, keepdims=True)
    s_new = s * jnp.exp(lw_ref[...]) + sa * b_ref[...] + v_col * k_ref[...]
    y_col = jnp.sum(s_new * r_ref[...], axis=-1, keepdims=True)
    y_ref[...] = jnp.sum(jnp.where(eye, y_col, 0.0), axis=-2, keepdims=True)
    s_out[...] = s_new


def _rwkv_step(r, lw, k, v, a, b, state, *, nb):
    n = state.shape[0]
    vec = lambda z: z.reshape(n, H_RWKV, 1, DH_RWKV)
    vspec = pl.BlockSpec((nb, H_RWKV, 1, DH_RWKV), lambda i: (i, 0, 0, 0))
    sspec = pl.BlockSpec((nb, H_RWKV, DH_RWKV, DH_RWKV), lambda i: (i, 0, 0, 0))
    y, s_new = pl.pallas_call(
        _rwkv_step_kernel,
        grid=(n // nb,),
        in_specs=[vspec] * 6 + [sspec],
        out_specs=[vspec, sspec],
        out_shape=[jax.ShapeDtypeStruct((n, H_RWKV, 1, DH_RWKV), F32),
                   jax.ShapeDtypeStruct(state.shape, F32)],
        compiler_params=_cparams(("parallel",)),
        name="rwkv_step",
    )(vec(r), vec(lw), vec(k), vec(v), vec(a), vec(b), state)
    return y.reshape(n, C_RWKV), s_new


def _rwkv_post_kernel(y_ref, g_ref, bonus_ref, lng_ref, lnb_ref, e_ref, et_ref, o_ref):
    y = y_ref[...]
    inv_n = 1.0 / DH_RWKV
    mean = _seg_sum(y, e_ref, et_ref) * inv_n
    yc = y - mean
    var = _seg_sum(yc * yc, e_ref, et_ref) * inv_n
    yn = yc * lax.rsqrt(var + LN_X_EPS) * lng_ref[...] + lnb_ref[...]
    o_ref[...] = ((yn + bonus_ref[...]) * g_ref[...]).astype(o_ref.dtype)


def _rwkv_post(y, g, bonus, ln_g, ln_b, sel, *, tm):
    m = y.shape[0]
    e, et = sel
    row = pl.BlockSpec((tm, C_RWKV), lambda i: (i, 0))
    vec = pl.BlockSpec((1, C_RWKV), lambda i: (0, 0))
    return pl.pallas_call(
        _rwkv_post_kernel,
        grid=(m // tm,),
        in_specs=[row, row, row, vec, vec,
                  pl.BlockSpec(e.shape, lambda i: (0, 0)), pl.BlockSpec(et.shape, lambda i: (0, 0))],
        out_specs=row,
        out_shape=jax.ShapeDtypeStruct((m, C_RWKV), BF16),
        compiler_params=_cparams(("parallel",)),
        name="rwkv_post",
    )(y, g, bonus, ln_g, ln_b, e, et)


def _pad_cols(w, width):
    return jnp.pad(w, ((0, 0), (0, width - w.shape[1])))


def _pad_rows(w, height):
    return jnp.pad(w, ((0, height - w.shape[0]), (0, 0)))


def _repack_rwkv_cols(w):
    wl0 = 3 * C_RWKV
    al0 = wl0 + W_LORA
    gl0 = al0 + A_LORA
    return jnp.concatenate([
        w[:, :wl0],
        _pad_cols(w[:, wl0:al0], LANES),
        _pad_cols(w[:, al0:gl0], LANES),
        _pad_cols(w[:, gl0:], 2 * LANES),
    ], axis=1)


def _unpack_rwkv_cols(w):
    return jnp.concatenate([
        w[:, :RP_WL], w[:, RP_WL:RP_WL + W_LORA], w[:, RP_AL:RP_AL + A_LORA], w[:, RP_GL:RP_GL + G_LORA],
    ], axis=1)


def _head_selectors():
    lane_head = jnp.arange(C_RWKV) // DH_RWKV
    e = (lane_head[:, None] == jnp.arange(LANES)[None, :]).astype(BF16)
    return e, e.T


def kernel(x_prompt, x_sample, cache_k, cache_v, cache_logf, state_wkv, state_shift, page_table,
           p_prompt, p_sample,
           ffn1_norm, ffn1_w_gate, ffn1_w_up, ffn1_w_down, mix_norm, w_in, fox_b_f,
           rwkv_mu, rwkv_w0, rwkv_w2, rwkv_a0, rwkv_a2, rwkv_g2, rwkv_k_k, rwkv_k_a, rwkv_r_k,
           rwkv_ln_g, rwkv_ln_b, w_o_fox, w_o_rwkv, w_out,
           ffn2_norm, ffn2_w_gate, ffn2_w_up, ffn2_w_down, ple_norm, ple_w_gate, ple_w_proj,
           final_norm):
    depth = w_in.shape[0]
    n_pr, s_len, d_model = x_prompt.shape
    n_dec = x_sample.shape[0]
    assert n_pr == 1 and x_sample.shape[1] == 1
    m_rows = s_len + n_dec
    n_pool, page = cache_k.shape[1:3]
    tm = 640 if m_rows % 640 == 0 else LANES
    tq = min(512, s_len)
    t_prep = min(256, s_len)

    off_f = 3 * C_FOX
    off_r = off_f + H_FOX
    off_g = off_r + N_RWKV
    qkv_scale = jnp.concatenate([jnp.full((1, C_FOX), DH_FOX ** -0.5, F32), jnp.ones((1, 2 * C_FOX), F32)], axis=1)
    sel = _head_selectors()
    row2 = lambda z: z.reshape(1, -1).astype(F32)

    x = jnp.concatenate([x_prompt[0], x_sample[:, 0]], axis=0)
    outs = {name: [] for name in ("kp", "vp", "lfp", "wkvp", "shp", "ks", "vs", "lfs", "wkvs", "shs")}
    for i in range(depth):
        bf = lambda w: w[i].astype(BF16)
        w_i = w_in[i]
        w_qkv = w_i[:, :off_f].astype(BF16)
        w_f = _pad_cols(w_i[:, off_f:off_r], LANES).astype(BF16)
        w_r = _repack_rwkv_cols(w_i[:, off_r:off_g]).astype(BF16)
        w_g = w_i[:, off_g:].astype(BF16)
        b_f = _pad_cols(row2(fox_b_f[i]), LANES)
        mix_g = row2(mix_norm[i])

        x = _ffn(x, row2(ffn1_norm[i]), bf(ffn1_w_gate), bf(ffn1_w_up), bf(ffn1_w_down), tm=tm, tf=512)

        qkv, qkv_bf = _proj(x, mix_g, w_qkv, qkv_scale, mode="qkv", tm=tm, tn=1024)
        logf = _proj(x, mix_g, w_f, b_f, mode="logf", tm=tm, tn=LANES)[0][:, :H_FOX]
        proj_r = _proj(x, mix_g, w_r, jnp.zeros((1, RP_W), F32), mode="plain", tm=tm, tn=512)[0]
        gates = _proj(x, mix_g, w_g, jnp.zeros((1, 2 * d_model), F32), mode="sigmoid", tm=tm, tn=1024)[0]

        lf_p = logf[:s_len]
        c_row = _cumsum_rows(lf_p.T)
        o_fox_p = _fox_prompt(qkv_bf, c_row.T, c_row, s_len=s_len, tq=tq, tk=tq)
        q_s = qkv[s_len:, :C_FOX].reshape(n_dec, 1, C_FOX)
        k_s = qkv[s_len:, C_FOX:2 * C_FOX].reshape(n_dec, 1, C_FOX)
        v_s = qkv[s_len:, 2 * C_FOX:].reshape(n_dec, 1, C_FOX)
        lf_s = logf[s_len:]
        o_fox_s = _fox_decode(
            page_table, q_s, k_s, v_s, lf_s.reshape(n_dec, H_FOX, 1),
            cache_k[i].reshape(n_pool, page, C_FOX), cache_v[i].reshape(n_pool, page, C_FOX),
            jnp.swapaxes(cache_logf[i], 1, 2))
        o_fox = jnp.concatenate([o_fox_p, o_fox_s[:, 0]], axis=0)

        vecs = (_repack_rwkv_cols(row2(rwkv_mu[i])), row2(rwkv_w0[i]), row2(rwkv_a0[i]),
                row2(rwkv_k_k[i]), row2(rwkv_k_a[i]), row2(rwkv_r_k[i]))
        mats = (_pad_rows(rwkv_w2[i], LANES).astype(BF16), _pad_rows(rwkv_a2[i], LANES).astype(BF16),
                _pad_rows(rwkv_g2[i], 2 * LANES).astype(BF16))
        proj_rp, proj_rs = proj_r[:s_len], proj_r[s_len:]
        pre_p = _rwkv_prep(proj_rp, proj_rp, vecs, mats, sel, sequence=True, tm=t_prep)
        prev_s = _repack_rwkv_cols(state_shift[i].astype(F32))
        pre_s = _rwkv_prep(proj_rs, prev_s, vecs, mats, sel, sequence=False, tm=n_dec)
        y_p, wkv_p = _rwkv_scan(*pre_p[:6], passes=SCAN_PASSES)
        y_s, wkv_s = _rwkv_step(*pre_s[:6], state_wkv[i].astype(F32), nb=8)
        cat = lambda a, b: jnp.concatenate([a, b], axis=0)
        o_rwkv = _rwkv_post(cat(y_p, y_s), cat(pre_p[6], pre_s[6]), cat(pre_p[7], pre_s[7]),
                            row2(rwkv_ln_g[i]), row2(rwkv_ln_b[i]), sel, tm=tm)

        x = _merge(x, o_fox, o_rwkv, gates, bf(w_o_fox), bf(w_o_rwkv), bf(w_out), tm=tm, tn=512)
        x = _ffn(x, row2(ffn2_norm[i]), bf(ffn2_w_gate), bf(ffn2_w_up), bf(ffn2_w_down), tm=tm, tf=512)
        p_all = jnp.concatenate([p_prompt[i, 0], p_sample[i, :, 0]], axis=0)
        x = _ple(x, row2(ple_norm[i]), p_all, bf(ple_w_gate), bf(ple_w_proj), tm=tm, tn=512)

        outs["kp"].append(qkv[:s_len, C_FOX:2 * C_FOX].reshape(1, s_len, H_FOX, DH_FOX))
        outs["vp"].append(qkv[:s_len, 2 * C_FOX:].reshape(1, s_len, H_FOX, DH_FOX))
        outs["lfp"].append(lf_p.reshape(1, s_len, H_FOX))
        outs["wkvp"].append(wkv_p[None])
        outs["shp"].append(_unpack_rwkv_cols(proj_rp[s_len - 1:s_len]))
        outs["ks"].append(k_s.reshape(n_dec, 1, H_FOX, DH_FOX))
        outs["vs"].append(v_s.reshape(n_dec, 1, H_FOX, DH_FOX))
        outs["lfs"].append(lf_s.reshape(n_dec, 1, H_FOX))
        outs["wkvs"].append(wkv_s)
        outs["shs"].append(_unpack_rwkv_cols(proj_rs))

    y = _final_norm(x, row2(final_norm), tm=tm)
    st = {k: jnp.stack(v) for k, v in outs.items()}
    return (y[:s_len].reshape(1, s_len, d_model), y[s_len:].reshape(n_dec, 1, d_model),
            st["kp"], st["vp"], st["lfp"], st["wkvp"], st["shp"],
            st["ks"], st["vs"], st["lfs"], st["wkvs"], st["shs"])
```

```python
import functools

import jax
import jax.numpy as jnp
from jax import lax
from jax.experimental import pallas as pl
from jax.experimental.pallas import tpu as pltpu

F32 = jnp.float32
BF16 = jnp.bfloat16

NORM_EPS = 1e-6
LN_X_EPS = 64e-5
KK_EPS = 1e-12

LANES = 128
VMEM_LIMIT = 56 * 1024 * 1024

H_FOX = 8
DH_FOX = 128
C_FOX = H_FOX * DH_FOX
H_RWKV = 16
DH_RWKV = 64
C_RWKV = H_RWKV * DH_RWKV
W_LORA = 64
A_LORA = 64
G_LORA = 160
N_RWKV = 3 * C_RWKV + W_LORA + A_LORA + G_LORA
RP_WL = 3 * C_RWKV
RP_AL = RP_WL + LANES
RP_GL = RP_AL + LANES
RP_W = RP_GL + 2 * LANES
LOG2E = 1.4426950408889634
DECODE_PAGES_PER_STEP = 8
SCAN_CHUNK = 64


def _cparams(sem):
    return pltpu.CompilerParams(dimension_semantics=sem, vmem_limit_bytes=VMEM_LIMIT)


def _sigmoid(x):
    return 1.0 / (1.0 + jnp.exp(-x))


def _softplus(x):
    return jnp.maximum(x, 0.0) + jnp.log(1.0 + jnp.exp(-jnp.abs(x)))


def _rms_rows(x, g):
    ms = jnp.mean(x * x, axis=-1, keepdims=True)
    return x * lax.rsqrt(ms + NORM_EPS) * g


def _dot(a, b):
    return jnp.dot(a, b, preferred_element_type=F32)


def _split2(x):
    hi = x.astype(BF16)
    lo = (x - hi.astype(F32)).astype(BF16)
    return hi, lo


def _ffn_kernel(x_ref, g_ref, wg_ref, wu_ref, wd_ref, o_ref, h_scr, acc_scr):
    f = pl.program_id(1)

    @pl.when(f == 0)
    def _():
        h_scr[...] = _rms_rows(x_ref[...], g_ref[...]).astype(BF16)
        acc_scr[...] = jnp.zeros_like(acc_scr)

    h = h_scr[...]
    gate = _dot(h, wg_ref[...])
    up = _dot(h, wu_ref[...])
    act = (gate * _sigmoid(gate) * up).astype(BF16)
    acc_scr[...] += _dot(act, wd_ref[...])

    @pl.when(f == pl.num_programs(1) - 1)
    def _():
        o_ref[...] = x_ref[...] + 0.5 * acc_scr[...]


def _ffn(x, g, wg, wu, wd, *, tm, tf):
    m, d = x.shape
    dff = wg.shape[1]
    return pl.pallas_call(
        _ffn_kernel,
        grid=(m // tm, dff // tf),
        in_specs=[
            pl.BlockSpec((tm, d), lambda i, f: (i, 0)),
            pl.BlockSpec((1, d), lambda i, f: (0, 0)),
            pl.BlockSpec((d, tf), lambda i, f: (0, f)),
            pl.BlockSpec((d, tf), lambda i, f: (0, f)),
            pl.BlockSpec((tf, d), lambda i, f: (f, 0)),
        ],
        out_specs=pl.BlockSpec((tm, d), lambda i, f: (i, 0)),
        out_shape=jax.ShapeDtypeStruct((m, d), F32),
        scratch_shapes=[pltpu.VMEM((tm, d), BF16), pltpu.VMEM((tm, d), F32)],
        compiler_params=_cparams(("parallel", "arbitrary")),
        name="ffn",
    )(x, g, wg, wu, wd)


def _proj_kernel(x_ref, g_ref, w_ref, aux_ref, *rest, mode):
    outs, h_scr = rest[:-1], rest[-1]

    @pl.when(pl.program_id(1) == 0)
    def _():
        h_scr[...] = _rms_rows(x_ref[...], g_ref[...]).astype(BF16)

    acc = _dot(h_scr[...], w_ref[...])
    if mode == "qkv":
        outs[0][...] = acc
        outs[1][...] = (acc * aux_ref[...]).astype(BF16)
    elif mode == "logf":
        outs[0][...] = -_softplus(-(acc + aux_ref[...]))
    elif mode == "sigmoid":
        outs[0][...] = _sigmoid(acc)
    else:
        outs[0][...] = acc


def _proj(x, g, w, aux, *, mode, tm, tn):
    m, d = x.shape
    n = w.shape[1]
    out_shape = [jax.ShapeDtypeStruct((m, n), F32)]
    if mode == "qkv":
        out_shape.append(jax.ShapeDtypeStruct((m, n), BF16))
    o_spec = pl.BlockSpec((tm, tn), lambda i, j: (i, j))
    return pl.pallas_call(
        functools.partial(_proj_kernel, mode=mode),
        grid=(m // tm, n // tn),
        in_specs=[
            pl.BlockSpec((tm, d), lambda i, j: (i, 0)),
            pl.BlockSpec((1, d), lambda i, j: (0, 0)),
            pl.BlockSpec((d, tn), lambda i, j: (0, j)),
            pl.BlockSpec((1, tn), lambda i, j: (0, j)),
        ],
        out_specs=[o_spec] * len(out_shape),
        out_shape=out_shape,
        scratch_shapes=[pltpu.VMEM((tm, d), BF16)],
        compiler_params=_cparams(("parallel", "arbitrary")),
        name="proj_" + mode,
    )(x, g, w, aux)


def _merge_kernel(x_ref, of_ref, or_ref, gf_ref, gr_ref, wof_ref, wor_ref, wout_ref, o_ref, acc_scr):
    n = pl.program_id(1)

    @pl.when(n == 0)
    def _():
        acc_scr[...] = jnp.zeros_like(acc_scr)

    merged = gf_ref[...] * _dot(of_ref[...], wof_ref[...]) + gr_ref[...] * _dot(or_ref[...], wor_ref[...])
    acc_scr[...] += _dot(merged.astype(BF16), wout_ref[...])

    @pl.when(n == pl.num_programs(1) - 1)
    def _():
        o_ref[...] = x_ref[...] + acc_scr[...]


def _merge(x, o_fox, o_rwkv, gates, wof, wor, wout, *, tm, tn):
    m, d = x.shape
    nb = d // tn
    return pl.pallas_call(
        _merge_kernel,
        grid=(m // tm, nb),
        in_specs=[
            pl.BlockSpec((tm, d), lambda i, n: (i, 0)),
            pl.BlockSpec((tm, o_fox.shape[1]), lambda i, n: (i, 0)),
            pl.BlockSpec((tm, o_rwkv.shape[1]), lambda i, n: (i, 0)),
            pl.BlockSpec((tm, tn), lambda i, n: (i, n)),
            pl.BlockSpec((tm, tn), lambda i, n: (i, n + nb)),
            pl.BlockSpec((wof.shape[0], tn), lambda i, n: (0, n)),
            pl.BlockSpec((wor.shape[0], tn), lambda i, n: (0, n)),
            pl.BlockSpec((tn, d), lambda i, n: (n, 0)),
        ],
        out_specs=pl.BlockSpec((tm, d), lambda i, n: (i, 0)),
        out_shape=jax.ShapeDtypeStruct((m, d), F32),
        scratch_shapes=[pltpu.VMEM((tm, d), F32)],
        compiler_params=_cparams(("parallel", "arbitrary")),
        name="merge",
    )(x, o_fox, o_rwkv, gates, gates, wof, wor, wout)


def _ple_kernel(x_ref, xc_ref, g_ref, p_ref, wg_ref, wp_ref, o_ref, u_scr):
    @pl.when(pl.program_id(1) == 0)
    def _():
        u_scr[...] = _rms_rows(x_ref[...], g_ref[...]).astype(BF16)

    gate = _sigmoid(_dot(u_scr[...], wg_ref[...]))
    emb = _dot(p_ref[...].astype(BF16), wp_ref[...])
    o_ref[...] = xc_ref[...] + gate * emb


def _ple(x, g, p, wg, wp, *, tm, tn):
    m, d = x.shape
    return pl.pallas_call(
        _ple_kernel,
        grid=(m // tm, d // tn),
        in_specs=[
            pl.BlockSpec((tm, d), lambda i, j: (i, 0)),
            pl.BlockSpec((tm, tn), lambda i, j: (i, j)),
            pl.BlockSpec((1, d), lambda i, j: (0, 0)),
            pl.BlockSpec((tm, p.shape[1]), lambda i, j: (i, 0)),
            pl.BlockSpec((d, tn), lambda i, j: (0, j)),
            pl.BlockSpec((p.shape[1], tn), lambda i, j: (0, j)),
        ],
        out_specs=pl.BlockSpec((tm, tn), lambda i, j: (i, j)),
        out_shape=jax.ShapeDtypeStruct((m, d), F32),
        scratch_shapes=[pltpu.VMEM((tm, d), BF16)],
        compiler_params=_cparams(("parallel", "arbitrary")),
        name="ple",
    )(x, x, g, p, wg, wp)


def _norm_kernel(x_ref, g_ref, o_ref):
    o_ref[...] = _rms_rows(x_ref[...], g_ref[...])


def _final_norm(x, g, *, tm):
    m, d = x.shape
    return pl.pallas_call(
        _norm_kernel,
        grid=(m // tm,),
        in_specs=[pl.BlockSpec((tm, d), lambda i: (i, 0)), pl.BlockSpec((1, d), lambda i: (0, 0))],
        out_specs=pl.BlockSpec((tm, d), lambda i: (i, 0)),
        out_shape=jax.ShapeDtypeStruct((m, d), F32),
        compiler_params=_cparams(("parallel",)),
        name="final_norm",
    )(x, g)


def _cumsum_kernel(lf_ref, o_ref):
    h, s = lf_ref.shape
    row = lax.broadcasted_iota(jnp.int32, (LANES, LANES), 0)
    col = lax.broadcasted_iota(jnp.int32, (LANES, LANES), 1)
    tri = jnp.where(row <= col, 1.0, 0.0)

    def body(c, carry):
        off = pl.multiple_of(c * LANES, LANES)
        y = jnp.dot(lf_ref[:, pl.ds(off, LANES)], tri, precision=lax.Precision.HIGHEST,
                    preferred_element_type=F32) + carry
        o_ref[:, pl.ds(off, LANES)] = y
        return y[:, LANES - 1:LANES]

    lax.fori_loop(0, s // LANES, body, jnp.zeros((h, 1), F32))


def _cumsum_rows(lf_t):
    return pl.pallas_call(
        _cumsum_kernel,
        out_shape=jax.ShapeDtypeStruct(lf_t.shape, F32),
        compiler_params=pltpu.CompilerParams(vmem_limit_bytes=VMEM_LIMIT),
        name="logf_cumsum",
    )(lf_t)


def _fox_prompt_kernel(q_ref, k_ref, v_ref, cq_ref, ck_ref, o_ref, m_scr, l_scr, acc_scr, *, tq, tk):
    qi = pl.program_id(0)
    kj = pl.program_id(1)

    @pl.when(kj == 0)
    def _():
        m_scr[...] = jnp.full_like(m_scr, -jnp.inf)
        l_scr[...] = jnp.zeros_like(l_scr)
        acc_scr[...] = jnp.zeros_like(acc_scr)

    def update(on_diagonal):
        if on_diagonal:
            causal = (lax.broadcasted_iota(jnp.int32, (tq, tk), 1)
                      <= lax.broadcasted_iota(jnp.int32, (tq, tk), 0))
        for h in range(H_FOX):
            sl = slice(h * DH_FOX, (h + 1) * DH_FOX)
            t = lax.dot_general(q_ref[:, sl], k_ref[:, sl], (((1,), (1,)), ((), ())),
                                preferred_element_type=F32)
            t = t - ck_ref[h:h + 1, :] * LOG2E
            if on_diagonal:
                t = jnp.where(causal, t, -jnp.inf)
            cq = cq_ref[:, h:h + 1] * LOG2E
            m_old = m_scr[h]
            m_new = jnp.maximum(m_old, jnp.max(t, axis=-1, keepdims=True) + cq)
            alpha = jnp.exp2(m_old - m_new)
            p = jnp.exp2(t - jnp.tile(m_new - cq, (1, tk // LANES)))
            l_scr[h] = alpha * l_scr[h] + jnp.sum(p, axis=-1, keepdims=True)
            acc_scr[h] = alpha * acc_scr[h] + _dot(p.astype(BF16), v_ref[:, sl])
            m_scr[h] = m_new

    @pl.when(kj < qi)
    def _():
        update(False)

    @pl.when(kj == qi)
    def _():
        update(True)
        for h in range(H_FOX):
            o_ref[:, h * DH_FOX:(h + 1) * DH_FOX] = (acc_scr[h] / l_scr[h]).astype(o_ref.dtype)


def _fox_prompt(qkv_bf, c_col, c_row, *, s_len, tq, tk):
    assert tq == tk and DH_FOX == LANES
    nq, nk = s_len // tq, s_len // tk
    return pl.pallas_call(
        functools.partial(_fox_prompt_kernel, tq=tq, tk=tk),
        grid=(nq, nk),
        in_specs=[
            pl.BlockSpec((tq, C_FOX), lambda i, j: (i, 0)),
            pl.BlockSpec((tk, C_FOX), lambda i, j: (jnp.minimum(i, j), 1)),
            pl.BlockSpec((tk, C_FOX), lambda i, j: (jnp.minimum(i, j), 2)),
            pl.BlockSpec((tq, H_FOX), lambda i, j: (i, 0)),
            pl.BlockSpec((H_FOX, tk), lambda i, j: (0, jnp.minimum(i, j))),
        ],
        out_specs=pl.BlockSpec((tq, C_FOX), lambda i, j: (i, 0)),
        out_shape=jax.ShapeDtypeStruct((s_len, C_FOX), BF16),
        scratch_shapes=[
            pltpu.VMEM((H_FOX, tq, LANES), F32),
            pltpu.VMEM((H_FOX, tq, LANES), F32),
            pltpu.VMEM((H_FOX, tq, DH_FOX), F32),
        ],
        compiler_params=_cparams(("parallel", "arbitrary")),
        name="fox_prompt",
    )(qkv_bf, qkv_bf, qkv_bf, c_col, c_row)


def _fox_decode_kernel(pt_ref, q_ref, kn_ref, vn_ref, lfn_ref, sfx_ref, *rest, scale, pps):
    del pt_ref
    kc, vc, lfc = rest[:pps], rest[pps:2 * pps], rest[2 * pps:3 * pps]
    o_ref, m_scr, l_scr, acc_scr, carry_scr = rest[3 * pps:]
    p = pl.program_id(1)
    page = kc[0].shape[0]
    nrow = page * H_FOX
    sub = lax.broadcasted_iota(jnp.int32, (H_FOX, nrow + LANES), 0)
    lane = lax.broadcasted_iota(jnp.int32, (H_FOX, nrow + LANES), 1)
    own_ext = (lane % H_FOX) == sub
    own = own_ext[:, :nrow]
    pick = lane[:, :LANES] == sub[:, :LANES]
    q8 = q_ref[...] * scale

    def to_col(row):
        return jnp.sum(jnp.where(pick, row, 0.0), axis=-1, keepdims=True)

    def per_head(row, op):
        parts = [row[:, j * LANES:(j + 1) * LANES] for j in range(nrow // LANES)]
        while len(parts) > 1:
            parts = [op(parts[i], parts[i + 1]) for i in range(0, len(parts), 2)]
        acc = parts[0]
        shift = H_FOX
        while shift < LANES:
            acc = op(acc, pltpu.roll(acc, shift, axis=1))
            shift *= 2
        return acc

    @pl.when(p == 0)
    def _():
        m_scr[...] = jnp.full_like(m_scr, -jnp.inf)
        l_scr[...] = jnp.zeros_like(l_scr)
        acc_scr[...] = jnp.zeros_like(acc_scr)
        carry_scr[...] = lfn_ref[...]

    q8_bf = q8.astype(BF16)
    widen = lambda row: jnp.tile(row, (1, nrow // LANES))
    carry = carry_scr[...]
    scores = []
    for j in range(pps):
        k2 = kc[j][...].reshape(nrow, DH_FOX).astype(BF16)
        g = lax.dot_general(q8_bf, k2, (_NT, ((), ())), preferred_element_type=F32)
        s = jnp.sum(jnp.where(own, g, 0.0), axis=0, keepdims=True)

        lf = lfc[j][...]
        hi = lf.astype(BF16).astype(F32)
        mid = (lf - hi).astype(BF16).astype(F32)
        lo = lf - hi - mid
        lf3 = jnp.concatenate([hi, mid, lo], axis=0).astype(BF16)
        t = lax.dot_general(lf3, sfx_ref[...], (_TN, ((), ())), preferred_element_type=F32)
        bias = jnp.sum(jnp.where(own_ext, t, 0.0), axis=0, keepdims=True)
        scores.append(s + bias[:, :nrow] + widen(carry))
        carry = carry + bias[:, nrow:]
    carry_scr[...] = carry

    m_old = m_scr[...]
    m_new = m_old
    for s in scores:
        m_new = jnp.maximum(m_new, per_head(s, jnp.maximum))
    alpha = jnp.exp(m_old - m_new)
    m_wide = widen(m_new)
    l_new = alpha * l_scr[...]
    acc = to_col(alpha) * acc_scr[...]
    for j, s in enumerate(scores):
        pr = jnp.exp(s - m_wide)
        l_new = l_new + per_head(pr, jnp.add)
        p8 = jnp.where(own, pr, 0.0).astype(BF16)
        acc = acc + _dot(p8, vc[j][...].reshape(nrow, DH_FOX).astype(BF16))
    l_scr[...] = l_new
    acc_scr[...] = acc
    m_scr[...] = m_new

    @pl.when(p == pl.num_programs(1) - 1)
    def _():
        s_new = jnp.sum(q8 * kn_ref[...], axis=-1, keepdims=True)
        m_old = to_col(m_scr[...])
        m_fin = jnp.maximum(m_old, s_new)
        alpha = jnp.exp(m_old - m_fin)
        p_new = jnp.exp(s_new - m_fin)
        l_fin = alpha * to_col(l_scr[...]) + p_new
        o_ref[...] = (alpha * acc_scr[...] + p_new * vn_ref[...]) / l_fin


def _suffix_selector(page):
    r = jnp.arange(page * H_FOX + LANES)
    tok = jnp.arange(page)
    later = jnp.where(r[None, :] < page * H_FOX, tok[:, None] > r[None, :] // H_FOX, True)
    return jnp.tile(later.astype(BF16), (3, 1))


def _fox_decode(layer, page_table, q, k_new, v_new, lf_new, cache_k, cache_v, cache_lf, *, pps):
    n, n_pages = page_table.shape
    page = cache_k.shape[2]
    pt_flat = page_table.reshape(-1)
    sfx = _suffix_selector(page)

    def page_map(j, tail):
        def index_map(b, p, pt):
            return (layer, pt[b * n_pages + (n_pages - 1 - (p * pps + j))]) + tail
        return index_map

    row_spec = pl.BlockSpec((None, H_FOX, DH_FOX), lambda b, p, pt: (b, 0, 0))
    kv_specs = [pl.BlockSpec((None, None, page, H_FOX, DH_FOX), page_map(j, (0, 0, 0))) for j in range(pps)]
    lf_specs = [pl.BlockSpec((None, None, page, H_FOX), page_map(j, (0, 0))) for j in range(pps)]
    grid_spec = pltpu.PrefetchScalarGridSpec(
        num_scalar_prefetch=1,
        grid=(n, n_pages // pps),
        in_specs=[
            row_spec, row_spec, row_spec,
            pl.BlockSpec((None, 1, LANES), lambda b, p, pt: (b, 0, 0)),
            pl.BlockSpec(sfx.shape, lambda b, p, pt: (0, 0)),
        ] + kv_specs + kv_specs + lf_specs,
        out_specs=row_spec,
        scratch_shapes=[
            pltpu.VMEM((1, LANES), F32),
            pltpu.VMEM((1, LANES), F32),
            pltpu.VMEM((H_FOX, DH_FOX), F32),
            pltpu.VMEM((1, LANES), F32),
        ],
    )
    return pl.pallas_call(
        functools.partial(_fox_decode_kernel, scale=DH_FOX ** -0.5, pps=pps),
        grid_spec=grid_spec,
        out_shape=jax.ShapeDtypeStruct((n, H_FOX, DH_FOX), F32),
        compiler_params=_cparams(("parallel", "arbitrary")),
        name="fox_decode",
    )(pt_flat, q, k_new, v_new, lf_new, sfx, *([cache_k] * pps), *([cache_v] * pps), *([cache_lf] * pps))


def _seg_sum(x, e_ref, et_ref):
    hi, lo = _split2(x)
    s = _dot(hi, e_ref[...]) + _dot(lo, e_ref[...])
    shi, slo = _split2(s)
    return _dot(shi, et_ref[...]) + _dot(slo, et_ref[...])


def _rwkv_prep_kernel(pr_ref, pv_ref, mu_ref, w0_ref, a0_ref, kk_ref, ka_ref, rk_ref,
                      w2_ref, a2_ref, g2_ref, e_ref, et_ref,
                      r_out, lw_out, k_out, v_out, a_out, b_out, g_out, bonus_out, *, sequence):
    pr = pr_ref[...]
    if sequence:
        first = jnp.where(pl.program_id(0) == 0, 0.0, pv_ref[7:8, :])
        prev = jnp.where(lax.broadcasted_iota(jnp.int32, pr.shape, 0) == 0, first,
                         pltpu.roll(pr, 1, axis=0))
    else:
        prev = pv_ref[...]
    xm = pr + (prev - pr) * mu_ref[...]
    r = xm[:, 0:C_RWKV]
    k = xm[:, C_RWKV:2 * C_RWKV]
    v = xm[:, 2 * C_RWKV:3 * C_RWKV]
    wl = xm[:, RP_WL:RP_AL]
    al = xm[:, RP_AL:RP_GL]
    gl = xm[:, RP_GL:RP_W]

    w_log = -_softplus(-(w0_ref[...] + _dot(jnp.tanh(wl).astype(BF16), w2_ref[...]))) - 0.5
    a_sig = _sigmoid(a0_ref[...] + _dot(al.astype(BF16), a2_ref[...]))
    g = _dot(_sigmoid(gl).astype(BF16), g2_ref[...])

    kk = k * kk_ref[...]
    norm = jnp.sqrt(_seg_sum(kk * kk, e_ref, et_ref))
    kk = kk / jnp.maximum(norm, KK_EPS)
    k_mod = k * (1.0 + (a_sig - 1.0) * ka_ref[...])

    r_out[...] = r
    lw_out[...] = -jnp.exp(w_log)
    k_out[...] = k_mod
    v_out[...] = v
    a_out[...] = -kk
    b_out[...] = kk * a_sig
    g_out[...] = g
    bonus_out[...] = _seg_sum(r * k_mod * rk_ref[...], e_ref, et_ref) * v


def _rwkv_prep(proj, prev, vecs, mats, sel, *, sequence, tm):
    m = proj.shape[0]
    mu, w0, a0, k_k, k_a, r_k = vecs
    w2, a2, g2 = mats
    e, et = sel
    row = lambda i: (i, 0)
    fix = lambda i: (0, 0)
    if sequence:
        prev_spec = pl.BlockSpec((8, RP_W), lambda i: (jnp.maximum(i * (tm // 8) - 1, 0), 0))
    else:
        prev_spec = pl.BlockSpec((tm, RP_W), row)
    vec_spec = pl.BlockSpec((1, C_RWKV), fix)
    out = jax.ShapeDtypeStruct((m, C_RWKV), F32)
    return pl.pallas_call(
        functools.partial(_rwkv_prep_kernel, sequence=sequence),
        grid=(m // tm,),
        in_specs=[
            pl.BlockSpec((tm, RP_W), row), prev_spec, pl.BlockSpec((1, RP_W), fix),
            vec_spec, vec_spec, vec_spec, vec_spec, vec_spec,
            pl.BlockSpec(w2.shape, fix), pl.BlockSpec(a2.shape, fix), pl.BlockSpec(g2.shape, fix),
            pl.BlockSpec(e.shape, fix), pl.BlockSpec(et.shape, fix),
        ],
        out_specs=[pl.BlockSpec((tm, C_RWKV), row)] * 8,
        out_shape=[out] * 8,
        compiler_params=_cparams(("parallel",)),
        name="rwkv_prep_seq" if sequence else "rwkv_prep_batch",
    )(proj, prev, mu, w0, a0, k_k, k_a, r_k, w2, a2, g2, e, et)


_NN = ((1,), (0,))
_NT = ((1,), (1,))
_TN = ((0,), (0,))


def _mm(a, b, dims):
    return lax.dot_general(a, b, (dims, ((), ())), preferred_element_type=F32)


def _rwkv_scan_kernel(r_ref, lw_ref, k_ref, v_ref, a_ref, b_ref, y_ref, s_out, s_scr):
    c = pl.program_id(0)
    ch = r_ref.shape[0]
    heads = range(H_RWKV)

    @pl.when(c == 0)
    def _():
        s_scr[...] = jnp.zeros_like(s_scr)

    row = lax.broadcasted_iota(jnp.int32, (ch, ch), 0)
    col = lax.broadcasted_iota(jnp.int32, (ch, ch), 1)
    incl = row >= col
    strict = row > col

    def cut(x):
        return [x[:, h * DH_RWKV:(h + 1) * DH_RWKV].astype(BF16) for h in heads]

    lw = lw_ref[...]
    cs = jnp.dot(jnp.where(incl, 1.0, 0.0), lw, precision=lax.Precision.HIGHEST, preferred_element_type=F32)
    p_incl = jnp.exp(cs)
    p_inv = jnp.exp(-cs)
    p_excl = jnp.exp(cs - lw)
    rm = cut(r_ref[...] * p_incl)
    am = cut(a_ref[...] * p_excl)
    bm = cut(b_ref[...] * p_inv)
    km = cut(k_ref[...] * p_inv)
    vm = cut(v_ref[...])
    p_end = p_incl[ch - 1:ch, :]
    s0 = [s_scr[h] for h in heads]
    s0b = [s.astype(BF16) for s in s0]

    def masked(xs, ys, keep):
        return [jnp.where(keep, _mm(x, y, _NT), 0.0).astype(BF16) for x, y in zip(xs, ys)]

    l_ab = masked(am, bm, strict)
    l_ak = masked(am, km, strict)
    m_rb = masked(rm, bm, incl)
    m_rk = masked(rm, km, incl)
    u = [_mm(am[h], s0b[h], _NT) + _mm(l_ak[h], vm[h], _NN) for h in heads]
    y = [_mm(rm[h], s0b[h], _NT) + _mm(m_rk[h], vm[h], _NN) for h in heads]
    pw = l_ab
    u = [u[h] + _mm(pw[h], u[h].astype(BF16), _NN) for h in heads]
    span = 2
    while span < ch:
        pw = [_mm(pw[h], pw[h], _NN).astype(BF16) for h in heads]
        u = [u[h] + _mm(pw[h], u[h].astype(BF16), _NN) for h in heads]
        span *= 2
    ub = [x.astype(BF16) for x in u]
    y = [y[h] + _mm(m_rb[h], ub[h], _NN) for h in heads]
    y_ref[...] = jnp.concatenate(y, axis=1)
    for h in heads:
        s_new = s0[h] + _mm(ub[h], bm[h], _TN) + _mm(vm[h], km[h], _TN)
        s_scr[h] = s_new * p_end[:, h * DH_RWKV:(h + 1) * DH_RWKV]

    @pl.when(c == pl.num_programs(0) - 1)
    def _():
        s_out[...] = s_scr[...]


def _rwkv_scan(r, lw, k, v, a, b):
    t = r.shape[0]
    spec = pl.BlockSpec((SCAN_CHUNK, C_RWKV), lambda c: (c, 0))
    st = (H_RWKV, DH_RWKV, DH_RWKV)
    return pl.pallas_call(
        _rwkv_scan_kernel,
        grid=(t // SCAN_CHUNK,),
        in_specs=[spec] * 6,
        out_specs=[spec, pl.BlockSpec(st, lambda c: (0, 0, 0))],
        out_shape=[jax.ShapeDtypeStruct((t, C_RWKV), F32), jax.ShapeDtypeStruct(st, F32)],
        scratch_shapes=[pltpu.VMEM(st, F32)],
        compiler_params=_cparams(("arbitrary",)),
        name="rwkv_scan",
    )(r, lw, k, v, a, b)


def _rwkv_step_kernel(r_ref, lw_ref, k_ref, v_ref, a_ref, b_ref, s_ref, y_ref, s_out):
    s = s_ref[...]
    dv, dk = s.shape[-2:]
    eye = (lax.broadcasted_iota(jnp.int32, (dv, dk), 0) == lax.broadcasted_iota(jnp.int32, (dv, dk), 1))
    sa = jnp.sum(s * a_ref[...], axis=-1, keepdims=True)
    v_col = jnp.sum(jnp.where(eye, v_ref[...], 0.0), axis=-1, keepdims=True)
    s_new = s * jnp.exp(lw_ref[...]) + sa * b_ref[...] + v_col * k_ref[...]
    y_col = jnp.sum(s_new * r_ref[...], axis=-1, keepdims=True)
    y_ref[...] = jnp.sum(jnp.where(eye, y_col, 0.0), axis=-2, keepdims=True)
    s_out[...] = s_new


def _rwkv_step(r, lw, k, v, a, b, state, *, nb):
    n = state.shape[0]
    vec = lambda z: z.reshape(n, H_RWKV, 1, DH_RWKV)
    vspec = pl.BlockSpec((nb, H_RWKV, 1, DH_RWKV), lambda i: (i, 0, 0, 0))
    sspec = pl.BlockSpec((nb, H_RWKV, DH_RWKV, DH_RWKV), lambda i: (i, 0, 0, 0))
    y, s_new = pl.pallas_call(
        _rwkv_step_kernel,
        grid=(n // nb,),
        in_specs=[vspec] * 6 + [sspec],
        out_specs=[vspec, sspec],
        out_shape=[jax.ShapeDtypeStruct((n, H_RWKV, 1, DH_RWKV), F32),
                   jax.ShapeDtypeStruct(state.shape, F32)],
        compiler_params=_cparams(("parallel",)),
        name="rwkv_step",
    )(vec(r), vec(lw), vec(k), vec(v), vec(a), vec(b), state)
    return y.reshape(n, C_RWKV), s_new


def _rwkv_post_kernel(y_ref, g_ref, bonus_ref, lng_ref, lnb_ref, e_ref, et_ref, o_ref):
    y = y_ref[...]
    inv_n = 1.0 / DH_RWKV
    mean = _seg_sum(y, e_ref, et_ref) * inv_n
    yc = y - mean
    var = _seg_sum(yc * yc, e_ref, et_ref) * inv_n
    yn = yc * lax.rsqrt(var + LN_X_EPS) * lng_ref[...] + lnb_ref[...]
    o_ref[...] = ((yn + bonus_ref[...]) * g_ref[...]).astype(o_ref.dtype)


def _rwkv_post(y, g, bonus, ln_g, ln_b, sel, *, tm):
    m = y.shape[0]
    e, et = sel
    row = pl.BlockSpec((tm, C_RWKV), lambda i: (i, 0))
    vec = pl.BlockSpec((1, C_RWKV), lambda i: (0, 0))
    return pl.pallas_call(
        _rwkv_post_kernel,
        grid=(m // tm,),
        in_specs=[row, row, row, vec, vec,
                  pl.BlockSpec(e.shape, lambda i: (0, 0)), pl.BlockSpec(et.shape, lambda i: (0, 0))],
        out_specs=row,
        out_shape=jax.ShapeDtypeStruct((m, C_RWKV), BF16),
        compiler_params=_cparams(("parallel",)),
        name="rwkv_post",
    )(y, g, bonus, ln_g, ln_b, e, et)


def _pad_cols(w, width):
    return jnp.pad(w, ((0, 0), (0, width - w.shape[1])))


def _pad_rows(w, height):
    return jnp.pad(w, ((0, height - w.shape[0]), (0, 0)))


def _repack_rwkv_cols(w):
    wl0 = 3 * C_RWKV
    al0 = wl0 + W_LORA
    gl0 = al0 + A_LORA
    return jnp.concatenate([
        w[:, :wl0],
        _pad_cols(w[:, wl0:al0], LANES),
        _pad_cols(w[:, al0:gl0], LANES),
        _pad_cols(w[:, gl0:], 2 * LANES),
    ], axis=1)


def _unpack_rwkv_cols(w):
    return jnp.concatenate([
        w[:, :RP_WL], w[:, RP_WL:RP_WL + W_LORA], w[:, RP_AL:RP_AL + A_LORA], w[:, RP_GL:RP_GL + G_LORA],
    ], axis=1)


def _head_selectors():
    lane_head = jnp.arange(C_RWKV) // DH_RWKV
    e = (lane_head[:, None] == jnp.arange(LANES)[None, :]).astype(BF16)
    return e, e.T


def kernel(x_prompt, x_sample, cache_k, cache_v, cache_logf, state_wkv, state_shift, page_table,
           p_prompt, p_sample,
           ffn1_norm, ffn1_w_gate, ffn1_w_up, ffn1_w_down, mix_norm, w_in, fox_b_f,
           rwkv_mu, rwkv_w0, rwkv_w2, rwkv_a0, rwkv_a2, rwkv_g2, rwkv_k_k, rwkv_k_a, rwkv_r_k,
           rwkv_ln_g, rwkv_ln_b, w_o_fox, w_o_rwkv, w_out,
           ffn2_norm, ffn2_w_gate, ffn2_w_up, ffn2_w_down, ple_norm, ple_w_gate, ple_w_proj,
           final_norm):
    depth = w_in.shape[0]
    n_pr, s_len, d_model = x_prompt.shape
    n_dec = x_sample.shape[0]
    assert n_pr == 1 and x_sample.shape[1] == 1
    m_rows = s_len + n_dec
    n_pages = page_table.shape[1]
    pps = DECODE_PAGES_PER_STEP if n_pages % DECODE_PAGES_PER_STEP == 0 else 1
    tm = 640 if m_rows % 640 == 0 else LANES
    tq = min(512, s_len)
    t_prep = min(256, s_len)

    off_f = 3 * C_FOX
    off_r = off_f + H_FOX
    off_g = off_r + N_RWKV
    qkv_scale = jnp.concatenate([jnp.full((1, C_FOX), LOG2E * DH_FOX ** -0.5, F32), jnp.ones((1, 2 * C_FOX), F32)], axis=1)
    sel = _head_selectors()
    row2 = lambda z: z.reshape(1, -1).astype(F32)

    x = jnp.concatenate([x_prompt[0], x_sample[:, 0]], axis=0)
    outs = {name: [] for name in ("kp", "vp", "lfp", "wkvp", "shp", "ks", "vs", "lfs", "wkvs", "shs")}
    for i in range(depth):
        bf = lambda w: w[i].astype(BF16)
        w_i = w_in[i]
        w_qkv = w_i[:, :off_f].astype(BF16)
        w_f = _pad_cols(w_i[:, off_f:off_r], LANES).astype(BF16)
        w_r = _repack_rwkv_cols(w_i[:, off_r:off_g]).astype(BF16)
        w_g = w_i[:, off_g:].astype(BF16)
        b_f = _pad_cols(row2(fox_b_f[i]), LANES)
        mix_g = row2(mix_norm[i])

        x = _ffn(x, row2(ffn1_norm[i]), bf(ffn1_w_gate), bf(ffn1_w_up), bf(ffn1_w_down), tm=tm, tf=512)

        qkv, qkv_bf = _proj(x, mix_g, w_qkv, qkv_scale, mode="qkv", tm=tm, tn=1024)
        logf = _proj(x, mix_g, w_f, b_f, mode="logf", tm=tm, tn=LANES)[0][:, :H_FOX]
        proj_r = _proj(x, mix_g, w_r, jnp.zeros((1, RP_W), F32), mode="plain", tm=tm, tn=512)[0]
        gates = _proj(x, mix_g, w_g, jnp.zeros((1, 2 * d_model), F32), mode="sigmoid", tm=tm, tn=1024)[0]

        lf_p = logf[:s_len]
        c_row = _cumsum_rows(lf_p.T)
        o_fox_p = _fox_prompt(qkv_bf, c_row.T, c_row, s_len=s_len, tq=tq, tk=tq)
        q_s = qkv[s_len:, :C_FOX].reshape(n_dec, H_FOX, DH_FOX)
        k_s = qkv[s_len:, C_FOX:2 * C_FOX].reshape(n_dec, H_FOX, DH_FOX)
        v_s = qkv[s_len:, 2 * C_FOX:].reshape(n_dec, H_FOX, DH_FOX)
        lf_s = logf[s_len:]
        o_fox_s = _fox_decode(
            i, page_table, q_s, k_s, v_s, jnp.tile(lf_s, (1, LANES // H_FOX)).reshape(n_dec, 1, LANES),
            cache_k, cache_v, cache_logf, pps=pps)
        o_fox = jnp.concatenate([o_fox_p, o_fox_s.reshape(n_dec, C_FOX).astype(BF16)], axis=0)

        vecs = (_repack_rwkv_cols(row2(rwkv_mu[i])), row2(rwkv_w0[i]), row2(rwkv_a0[i]),
                row2(rwkv_k_k[i]), row2(rwkv_k_a[i]), row2(rwkv_r_k[i]))
        mats = (_pad_rows(rwkv_w2[i], LANES).astype(BF16), _pad_rows(rwkv_a2[i], LANES).astype(BF16),
                _pad_rows(rwkv_g2[i], 2 * LANES).astype(BF16))
        proj_rp, proj_rs = proj_r[:s_len], proj_r[s_len:]
        pre_p = _rwkv_prep(proj_rp, proj_rp, vecs, mats, sel, sequence=True, tm=t_prep)
        prev_s = _repack_rwkv_cols(state_shift[i].astype(F32))
        pre_s = _rwkv_prep(proj_rs, prev_s, vecs, mats, sel, sequence=False, tm=n_dec)
        y_p, wkv_p = _rwkv_scan(*pre_p[:6])
        y_s, wkv_s = _rwkv_step(*pre_s[:6], state_wkv[i].astype(F32), nb=8)
        cat = lambda a, b: jnp.concatenate([a, b], axis=0)
        o_rwkv = _rwkv_post(cat(y_p, y_s), cat(pre_p[6], pre_s[6]), cat(pre_p[7], pre_s[7]),
                            row2(rwkv_ln_g[i]), row2(rwkv_ln_b[i]), sel, tm=tm)

        x = _merge(x, o_fox, o_rwkv, gates, bf(w_o_fox), bf(w_o_rwkv), bf(w_out), tm=tm, tn=512)
        x = _ffn(x, row2(ffn2_norm[i]), bf(ffn2_w_gate), bf(ffn2_w_up), bf(ffn2_w_down), tm=tm, tf=512)
        p_all = jnp.concatenate([p_prompt[i, 0], p_sample[i, :, 0]], axis=0)
        x = _ple(x, row2(ple_norm[i]), p_all, bf(ple_w_gate), bf(ple_w_proj), tm=tm, tn=512)

        outs["kp"].append(qkv[:s_len, C_FOX:2 * C_FOX].reshape(1, s_len, H_FOX, DH_FOX))
        outs["vp"].append(qkv[:s_len, 2 * C_FOX:].reshape(1, s_len, H_FOX, DH_FOX))
        outs["lfp"].append(lf_p.reshape(1, s_len, H_FOX))
        outs["wkvp"].append(wkv_p[None])
        outs["shp"].append(_unpack_rwkv_cols(proj_rp[s_len - 1:s_len]))
        outs["ks"].append(k_s.reshape(n_dec, 1, H_FOX, DH_FOX))
        outs["vs"].append(v_s.reshape(n_dec, 1, H_FOX, DH_FOX))
        outs["lfs"].append(lf_s.reshape(n_dec, 1, H_FOX))
        outs["wkvs"].append(wkv_s)
        outs["shs"].append(_unpack_rwkv_cols(proj_rs))

    y = _final_norm(x, row2(final_norm), tm=tm)
    st = {k: jnp.stack(v) for k, v in outs.items()}
    return (y[:s_len].reshape(1, s_len, d_model), y[s_len:].reshape(n_dec, 1, d_model),
            st["kp"], st["vp"], st["lfp"], st["wkvp"], st["shp"],
            st["ks"], st["vs"], st["lfs"], st["wkvs"], st["shs"])
```

```python
import functools

import jax
import jax.numpy as jnp
from jax import lax
from jax.experimental import pallas as pl
from jax.experimental.pallas import tpu as pltpu

F32 = jnp.float32
BF16 = jnp.bfloat16

NORM_EPS = 1e-6
LN_X_EPS = 64e-5
KK_EPS = 1e-12

LANES = 128
VMEM_LIMIT = 56 * 1024 * 1024

H_FOX = 8
DH_FOX = 128
C_FOX = H_FOX * DH_FOX
H_RWKV = 16
DH_RWKV = 64
C_RWKV = H_RWKV * DH_RWKV
W_LORA = 64
A_LORA = 64
G_LORA = 160
N_RWKV = 3 * C_RWKV + W_LORA + A_LORA + G_LORA
RP_WL = 3 * C_RWKV
RP_AL = RP_WL + LANES
RP_GL = RP_AL + LANES
RP_W = RP_GL + 2 * LANES
LOG2E = 1.4426950408889634
DECODE_PAGES_PER_STEP = 8
SCAN_CHUNK = 64


def _cparams(sem):
    return pltpu.CompilerParams(dimension_semantics=sem, vmem_limit_bytes=VMEM_LIMIT)


def _sigmoid(x):
    return 1.0 / (1.0 + jnp.exp(-x))


def _softplus(x):
    return jnp.maximum(x, 0.0) + jnp.log(1.0 + jnp.exp(-jnp.abs(x)))


def _rms_rows(x, g):
    ms = jnp.mean(x * x, axis=-1, keepdims=True)
    return x * lax.rsqrt(ms + NORM_EPS) * g


def _dot(a, b):
    return jnp.dot(a, b, preferred_element_type=F32)


def _split2(x):
    hi = x.astype(BF16)
    lo = (x - hi.astype(F32)).astype(BF16)
    return hi, lo


def _ffn_kernel(x_ref, g_ref, gn_ref, wg_ref, wu_ref, wd_ref, o_ref, hn_ref, h_scr, acc_scr):
    f = pl.program_id(1)

    @pl.when(f == 0)
    def _():
        h_scr[...] = _rms_rows(x_ref[...], g_ref[...]).astype(BF16)
        acc_scr[...] = jnp.zeros_like(acc_scr)

    h = h_scr[...]
    gate = _dot(h, wg_ref[...])
    up = _dot(h, wu_ref[...])
    act = (gate * _sigmoid(gate) * up).astype(BF16)
    acc_scr[...] += _dot(act, wd_ref[...])

    @pl.when(f == pl.num_programs(1) - 1)
    def _():
        out = x_ref[...] + 0.5 * acc_scr[...]
        o_ref[...] = out
        hn_ref[...] = _rms_rows(out, gn_ref[...]).astype(BF16)


def _ffn(x, g, g_next, wg, wu, wd, *, tm, tf):
    m, d = x.shape
    dff = wg.shape[1]
    row = pl.BlockSpec((tm, d), lambda i, f: (i, 0))
    vec = pl.BlockSpec((1, d), lambda i, f: (0, 0))
    return pl.pallas_call(
        _ffn_kernel,
        grid=(m // tm, dff // tf),
        in_specs=[
            row, vec, vec,
            pl.BlockSpec((d, tf), lambda i, f: (0, f)),
            pl.BlockSpec((d, tf), lambda i, f: (0, f)),
            pl.BlockSpec((tf, d), lambda i, f: (f, 0)),
        ],
        out_specs=[row, row],
        out_shape=[jax.ShapeDtypeStruct((m, d), F32), jax.ShapeDtypeStruct((m, d), BF16)],
        scratch_shapes=[pltpu.VMEM((tm, d), BF16), pltpu.VMEM((tm, d), F32)],
        compiler_params=_cparams(("parallel", "arbitrary")),
        name="ffn",
    )(x, g, g_next, wg, wu, wd)


def _proj_kernel(h_ref, w_ref, aux_ref, *outs, mode):
    acc = _dot(h_ref[...], w_ref[...])
    if mode == "qkv":
        outs[0][...] = acc
        outs[1][...] = (acc * aux_ref[...]).astype(BF16)
    elif mode == "logf":
        outs[0][...] = -_softplus(-(acc + aux_ref[...]))
    elif mode == "sigmoid":
        outs[0][...] = _sigmoid(acc)
    else:
        outs[0][...] = acc


def _proj(h, w, aux, *, mode, tm, tn):
    m, d = h.shape
    n = w.shape[1]
    out_shape = [jax.ShapeDtypeStruct((m, n), F32)]
    if mode == "qkv":
        out_shape.append(jax.ShapeDtypeStruct((m, n), BF16))
    o_spec = pl.BlockSpec((tm, tn), lambda i, j: (i, j))
    return pl.pallas_call(
        functools.partial(_proj_kernel, mode=mode),
        grid=(m // tm, n // tn),
        in_specs=[
            pl.BlockSpec((tm, d), lambda i, j: (i, 0)),
            pl.BlockSpec((d, tn), lambda i, j: (0, j)),
            pl.BlockSpec((1, tn), lambda i, j: (0, j)),
        ],
        out_specs=[o_spec] * len(out_shape),
        out_shape=out_shape,
        compiler_params=_cparams(("parallel", "arbitrary")),
        name="proj_" + mode,
    )(h, w, aux)


def _merge_kernel(x_ref, of_ref, or_ref, gf_ref, gr_ref, wof_ref, wor_ref, wout_ref, o_ref, acc_scr):
    n = pl.program_id(1)

    @pl.when(n == 0)
    def _():
        acc_scr[...] = jnp.zeros_like(acc_scr)

    merged = gf_ref[...] * _dot(of_ref[...], wof_ref[...]) + gr_ref[...] * _dot(or_ref[...], wor_ref[...])
    acc_scr[...] += _dot(merged.astype(BF16), wout_ref[...])

    @pl.when(n == pl.num_programs(1) - 1)
    def _():
        o_ref[...] = x_ref[...] + acc_scr[...]


def _merge(x, o_fox, o_rwkv, gates, wof, wor, wout, *, tm, tn):
    m, d = x.shape
    nb = d // tn
    return pl.pallas_call(
        _merge_kernel,
        grid=(m // tm, nb),
        in_specs=[
            pl.BlockSpec((tm, d), lambda i, n: (i, 0)),
            pl.BlockSpec((tm, o_fox.shape[1]), lambda i, n: (i, 0)),
            pl.BlockSpec((tm, o_rwkv.shape[1]), lambda i, n: (i, 0)),
            pl.BlockSpec((tm, tn), lambda i, n: (i, n)),
            pl.BlockSpec((tm, tn), lambda i, n: (i, n + nb)),
            pl.BlockSpec((wof.shape[0], tn), lambda i, n: (0, n)),
            pl.BlockSpec((wor.shape[0], tn), lambda i, n: (0, n)),
            pl.BlockSpec((tn, d), lambda i, n: (n, 0)),
        ],
        out_specs=pl.BlockSpec((tm, d), lambda i, n: (i, 0)),
        out_shape=jax.ShapeDtypeStruct((m, d), F32),
        scratch_shapes=[pltpu.VMEM((tm, d), F32)],
        compiler_params=_cparams(("parallel", "arbitrary")),
        name="merge",
    )(x, o_fox, o_rwkv, gates, gates, wof, wor, wout)


def _ple_kernel(x_ref, u_ref, p_ref, wg_ref, wp_ref, o_ref):
    gate = _sigmoid(_dot(u_ref[...], wg_ref[...]))
    emb = _dot(p_ref[...].astype(BF16), wp_ref[...])
    o_ref[...] = x_ref[...] + gate * emb


def _ple(x, u, p, wg, wp, *, tm, tn):
    m, d = x.shape
    return pl.pallas_call(
        _ple_kernel,
        grid=(m // tm, d // tn),
        in_specs=[
            pl.BlockSpec((tm, tn), lambda i, j: (i, j)),
            pl.BlockSpec((tm, d), lambda i, j: (i, 0)),
            pl.BlockSpec((tm, p.shape[1]), lambda i, j: (i, 0)),
            pl.BlockSpec((d, tn), lambda i, j: (0, j)),
            pl.BlockSpec((p.shape[1], tn), lambda i, j: (0, j)),
        ],
        out_specs=pl.BlockSpec((tm, tn), lambda i, j: (i, j)),
        out_shape=jax.ShapeDtypeStruct((m, d), F32),
        compiler_params=_cparams(("parallel", "arbitrary")),
        name="ple",
    )(x, u, p, wg, wp)


def _norm_kernel(x_ref, g_ref, o_ref):
    o_ref[...] = _rms_rows(x_ref[...], g_ref[...])


def _final_norm(x, g, *, tm):
    m, d = x.shape
    return pl.pallas_call(
        _norm_kernel,
        grid=(m // tm,),
        in_specs=[pl.BlockSpec((tm, d), lambda i: (i, 0)), pl.BlockSpec((1, d), lambda i: (0, 0))],
        out_specs=pl.BlockSpec((tm, d), lambda i: (i, 0)),
        out_shape=jax.ShapeDtypeStruct((m, d), F32),
        compiler_params=_cparams(("parallel",)),
        name="final_norm",
    )(x, g)


def _cumsum_kernel(lf_ref, o_ref):
    h, s = lf_ref.shape
    row = lax.broadcasted_iota(jnp.int32, (LANES, LANES), 0)
    col = lax.broadcasted_iota(jnp.int32, (LANES, LANES), 1)
    tri = jnp.where(row <= col, 1.0, 0.0)

    def body(c, carry):
        off = pl.multiple_of(c * LANES, LANES)
        y = jnp.dot(lf_ref[:, pl.ds(off, LANES)], tri, precision=lax.Precision.HIGHEST,
                    preferred_element_type=F32) + carry
        o_ref[:, pl.ds(off, LANES)] = y
        return y[:, LANES - 1:LANES]

    lax.fori_loop(0, s // LANES, body, jnp.zeros((h, 1), F32))


def _cumsum_rows(lf_t):
    return pl.pallas_call(
        _cumsum_kernel,
        out_shape=jax.ShapeDtypeStruct(lf_t.shape, F32),
        compiler_params=pltpu.CompilerParams(vmem_limit_bytes=VMEM_LIMIT),
        name="logf_cumsum",
    )(lf_t)


def _fox_prompt_kernel(q_ref, k_ref, v_ref, qx_ref, kx_ref, o_ref, m_scr, l_scr, acc_scr, *, tq, tk):
    qi = pl.program_id(0)
    kj = pl.program_id(1)

    @pl.when(kj == 0)
    def _():
        m_scr[...] = jnp.full_like(m_scr, -jnp.inf)
        l_scr[...] = jnp.zeros_like(l_scr)
        acc_scr[...] = jnp.zeros_like(acc_scr)

    def update(on_diagonal):
        if on_diagonal:
            causal = (lax.broadcasted_iota(jnp.int32, (tq, tk), 1)
                      <= lax.broadcasted_iota(jnp.int32, (tq, tk), 0))
        heads = range(H_FOX)
        sl = [slice(h * DH_FOX, (h + 1) * DH_FOX) for h in heads]
        ones = jnp.ones((tk, LANES), BF16)
        t = []
        for h in heads:
            q_wide = jnp.concatenate([q_ref[:, sl[h]], qx_ref[h]], axis=1)
            k_wide = jnp.concatenate([k_ref[:, sl[h]], kx_ref[h]], axis=1)
            th = lax.dot_general(q_wide, k_wide, (_NT, ((), ())), preferred_element_type=F32)
            t.append(jnp.where(causal, th, -jnp.inf) if on_diagonal else th)
        m_old = [m_scr[h] for h in heads]
        m_new = [jnp.maximum(m_old[h], jnp.max(t[h], axis=-1, keepdims=True)) for h in heads]
        alpha = [jnp.exp2(m_old[h] - m_new[h]) for h in heads]
        p = [jnp.exp2(t[h] - jnp.tile(m_new[h], (1, tk // LANES))).astype(BF16) for h in heads]
        for h in heads:
            pv = _dot(p[h], jnp.concatenate([v_ref[:, sl[h]], ones], axis=1))
            acc_scr[h] = alpha[h] * acc_scr[h] + pv[:, :DH_FOX]
            l_scr[h] = alpha[h] * l_scr[h] + pv[:, DH_FOX:]
            m_scr[h] = m_new[h]

    @pl.when(kj < qi)
    def _():
        update(False)

    @pl.when(kj == qi)
    def _():
        update(True)
        for h in range(H_FOX):
            o_ref[:, h * DH_FOX:(h + 1) * DH_FOX] = (acc_scr[h] / l_scr[h]).astype(o_ref.dtype)


def _forget_blocks(c_col):
    def top16(x):
        bits = lax.bitcast_convert_type(x, jnp.uint32) & jnp.uint32(0xFFFF0000)
        return lax.bitcast_convert_type(bits, F32)

    c2 = (c_col * LOG2E).T[:, :, None]
    hi = top16(c2)
    mid = top16(c2 - hi)
    lo = c2 - hi - mid
    parts = jnp.concatenate([hi, mid, lo], axis=2)
    ones = jnp.ones_like(parts)
    pad = jnp.zeros(parts.shape[:2] + (LANES - 6,), F32)
    qx = jnp.concatenate([parts, ones, pad], axis=2).astype(BF16)
    kx = jnp.concatenate([ones, -parts, pad], axis=2).astype(BF16)
    return qx, kx


def _fox_prompt(qkv_bf, c_col, *, s_len, tq, tk):
    assert tq == tk and DH_FOX == LANES
    nq, nk = s_len // tq, s_len // tk
    qx, kx = _forget_blocks(c_col)
    return pl.pallas_call(
        functools.partial(_fox_prompt_kernel, tq=tq, tk=tk),
        grid=(nq, nk),
        in_specs=[
            pl.BlockSpec((tq, C_FOX), lambda i, j: (i, 0)),
            pl.BlockSpec((tk, C_FOX), lambda i, j: (jnp.minimum(i, j), 1)),
            pl.BlockSpec((tk, C_FOX), lambda i, j: (jnp.minimum(i, j), 2)),
            pl.BlockSpec((H_FOX, tq, LANES), lambda i, j: (0, i, 0)),
            pl.BlockSpec((H_FOX, tk, LANES), lambda i, j: (0, jnp.minimum(i, j), 0)),
        ],
        out_specs=pl.BlockSpec((tq, C_FOX), lambda i, j: (i, 0)),
        out_shape=jax.ShapeDtypeStruct((s_len, C_FOX), BF16),
        scratch_shapes=[
            pltpu.VMEM((H_FOX, tq, LANES), F32),
            pltpu.VMEM((H_FOX, tq, LANES), F32),
            pltpu.VMEM((H_FOX, tq, DH_FOX), F32),
        ],
        compiler_params=_cparams(("parallel", "arbitrary")),
        name="fox_prompt",
    )(qkv_bf, qkv_bf, qkv_bf, qx, kx)


def _fox_decode_kernel(pt_ref, q_ref, kn_ref, vn_ref, lfn_ref, sfx_ref, *rest, scale, pps):
    del pt_ref
    kc, vc, lfc = rest[:pps], rest[pps:2 * pps], rest[2 * pps:3 * pps]
    o_ref, m_scr, l_scr, acc_scr, carry_scr = rest[3 * pps:]
    p = pl.program_id(1)
    page = kc[0].shape[0]
    nrow = page * H_FOX
    sub = lax.broadcasted_iota(jnp.int32, (H_FOX, nrow + LANES), 0)
    lane = lax.broadcasted_iota(jnp.int32, (H_FOX, nrow + LANES), 1)
    own_ext = (lane % H_FOX) == sub
    own = own_ext[:, :nrow]
    pick = lane[:, :LANES] == sub[:, :LANES]
    q8 = q_ref[...] * scale

    def to_col(row):
        return jnp.sum(jnp.where(pick, row, 0.0), axis=-1, keepdims=True)

    def per_head(row, op):
        parts = [row[:, j * LANES:(j + 1) * LANES] for j in range(nrow // LANES)]
        while len(parts) > 1:
            parts = [op(parts[i], parts[i + 1]) for i in range(0, len(parts), 2)]
        acc = parts[0]
        shift = H_FOX
        while shift < LANES:
            acc = op(acc, pltpu.roll(acc, shift, axis=1))
            shift *= 2
        return acc

    @pl.when(p == 0)
    def _():
        m_scr[...] = jnp.full_like(m_scr, -jnp.inf)
        l_scr[...] = jnp.zeros_like(l_scr)
        acc_scr[...] = jnp.zeros_like(acc_scr)
        carry_scr[...] = lfn_ref[...]

    q8_bf = q8.astype(BF16)
    widen = lambda row: jnp.tile(row, (1, nrow // LANES))
    carry = carry_scr[...]
    scores = []
    for j in range(pps):
        k2 = kc[j][...].reshape(nrow, DH_FOX).astype(BF16)
        g = lax.dot_general(q8_bf, k2, (_NT, ((), ())), preferred_element_type=F32)
        s = jnp.sum(jnp.where(own, g, 0.0), axis=0, keepdims=True)

        lf = lfc[j][...]
        hi = lf.astype(BF16).astype(F32)
        mid = (lf - hi).astype(BF16).astype(F32)
        lo = lf - hi - mid
        lf3 = jnp.concatenate([hi, mid, lo], axis=0).astype(BF16)
        t = lax.dot_general(lf3, sfx_ref[...], (_TN, ((), ())), preferred_element_type=F32)
        bias = jnp.sum(jnp.where(own_ext, t, 0.0), axis=0, keepdims=True)
        scores.append(s + bias[:, :nrow] + widen(carry))
        carry = carry + bias[:, nrow:]
    carry_scr[...] = carry

    m_old = m_scr[...]
    m_new = m_old
    for s in scores:
        m_new = jnp.maximum(m_new, per_head(s, jnp.maximum))
    alpha = jnp.exp(m_old - m_new)
    m_wide = widen(m_new)
    l_new = alpha * l_scr[...]
    acc = to_col(alpha) * acc_scr[...]
    for j, s in enumerate(scores):
        pr = jnp.exp(s - m_wide)
        l_new = l_new + per_head(pr, jnp.add)
        p8 = jnp.where(own, pr, 0.0).astype(BF16)
        acc = acc + _dot(p8, vc[j][...].reshape(nrow, DH_FOX).astype(BF16))
    l_scr[...] = l_new
    acc_scr[...] = acc
    m_scr[...] = m_new

    @pl.when(p == pl.num_programs(1) - 1)
    def _():
        s_new = jnp.sum(q8 * kn_ref[...], axis=-1, keepdims=True)
        m_old = to_col(m_scr[...])
        m_fin = jnp.maximum(m_old, s_new)
        alpha = jnp.exp(m_old - m_fin)
        p_new = jnp.exp(s_new - m_fin)
        l_fin = alpha * to_col(l_scr[...]) + p_new
        o_ref[...] = (alpha * acc_scr[...] + p_new * vn_ref[...]) / l_fin


def _suffix_selector(page):
    r = jnp.arange(page * H_FOX + LANES)
    tok = jnp.arange(page)
    later = jnp.where(r[None, :] < page * H_FOX, tok[:, None] > r[None, :] // H_FOX, True)
    return jnp.tile(later.astype(BF16), (3, 1))


def _fox_decode(layer, page_table, q, k_new, v_new, lf_new, cache_k, cache_v, cache_lf, *, pps):
    n, n_pages = page_table.shape
    page = cache_k.shape[2]
    pt_flat = page_table.reshape(-1)
    sfx = _suffix_selector(page)

    def page_map(j, tail):
        def index_map(b, p, pt):
            return (layer, pt[b * n_pages + (n_pages - 1 - (p * pps + j))]) + tail
        return index_map

    row_spec = pl.BlockSpec((None, H_FOX, DH_FOX), lambda b, p, pt: (b, 0, 0))
    kv_specs = [pl.BlockSpec((None, None, page, H_FOX, DH_FOX), page_map(j, (0, 0, 0))) for j in range(pps)]
    lf_specs = [pl.BlockSpec((None, None, page, H_FOX), page_map(j, (0, 0))) for j in range(pps)]
    grid_spec = pltpu.PrefetchScalarGridSpec(
        num_scalar_prefetch=1,
        grid=(n, n_pages // pps),
        in_specs=[
            row_spec, row_spec, row_spec,
            pl.BlockSpec((None, 1, LANES), lambda b, p, pt: (b, 0, 0)),
            pl.BlockSpec(sfx.shape, lambda b, p, pt: (0, 0)),
        ] + kv_specs + kv_specs + lf_specs,
        out_specs=row_spec,
        scratch_shapes=[
            pltpu.VMEM((1, LANES), F32),
            pltpu.VMEM((1, LANES), F32),
            pltpu.VMEM((H_FOX, DH_FOX), F32),
            pltpu.VMEM((1, LANES), F32),
        ],
    )
    return pl.pallas_call(
        functools.partial(_fox_decode_kernel, scale=DH_FOX ** -0.5, pps=pps),
        grid_spec=grid_spec,
        out_shape=jax.ShapeDtypeStruct((n, H_FOX, DH_FOX), F32),
        compiler_params=_cparams(("parallel", "arbitrary")),
        name="fox_decode",
    )(pt_flat, q, k_new, v_new, lf_new, sfx, *([cache_k] * pps), *([cache_v] * pps), *([cache_lf] * pps))


def _seg_sum(x, e_ref, et_ref):
    hi, lo = _split2(x)
    s = _dot(hi, e_ref[...]) + _dot(lo, e_ref[...])
    shi, slo = _split2(s)
    return _dot(shi, et_ref[...]) + _dot(slo, et_ref[...])


def _rwkv_prep_kernel(pr_ref, pv_ref, mu_ref, w0_ref, a0_ref, kk_ref, ka_ref, rk_ref,
                      w2_ref, a2_ref, g2_ref, e_ref, et_ref,
                      r_out, lw_out, k_out, v_out, a_out, b_out, g_out, bonus_out, *, sequence):
    pr = pr_ref[...]
    if sequence:
        first = jnp.where(pl.program_id(0) == 0, 0.0, pv_ref[7:8, :])
        prev = jnp.where(lax.broadcasted_iota(jnp.int32, pr.shape, 0) == 0, first,
                         pltpu.roll(pr, 1, axis=0))
    else:
        prev = pv_ref[...]
    xm = pr + (prev - pr) * mu_ref[...]
    r = xm[:, 0:C_RWKV]
    k = xm[:, C_RWKV:2 * C_RWKV]
    v = xm[:, 2 * C_RWKV:3 * C_RWKV]
    wl = xm[:, RP_WL:RP_AL]
    al = xm[:, RP_AL:RP_GL]
    gl = xm[:, RP_GL:RP_W]

    w_log = -_softplus(-(w0_ref[...] + _dot(jnp.tanh(wl).astype(BF16), w2_ref[...]))) - 0.5
    a_sig = _sigmoid(a0_ref[...] + _dot(al.astype(BF16), a2_ref[...]))
    g = _dot(_sigmoid(gl).astype(BF16), g2_ref[...])

    kk = k * kk_ref[...]
    norm = jnp.sqrt(_seg_sum(kk * kk, e_ref, et_ref))
    kk = kk / jnp.maximum(norm, KK_EPS)
    k_mod = k * (1.0 + (a_sig - 1.0) * ka_ref[...])

    r_out[...] = r
    lw_out[...] = -jnp.exp(w_log)
    k_out[...] = k_mod
    v_out[...] = v
    a_out[...] = -kk
    b_out[...] = kk * a_sig
    g_out[...] = g
    bonus_out[...] = _seg_sum(r * k_mod * rk_ref[...], e_ref, et_ref) * v


def _rwkv_prep(proj, prev, vecs, mats, sel, *, sequence, tm):
    m = proj.shape[0]
    mu, w0, a0, k_k, k_a, r_k = vecs
    w2, a2, g2 = mats
    e, et = sel
    row = lambda i: (i, 0)
    fix = lambda i: (0, 0)
    if sequence:
        prev_spec = pl.BlockSpec((8, RP_W), lambda i: (jnp.maximum(i * (tm // 8) - 1, 0), 0))
    else:
        prev_spec = pl.BlockSpec((tm, RP_W), row)
    vec_spec = pl.BlockSpec((1, C_RWKV), fix)
    out = jax.ShapeDtypeStruct((m, C_RWKV), F32)
    return pl.pallas_call(
        functools.partial(_rwkv_prep_kernel, sequence=sequence),
        grid=(m // tm,),
        in_specs=[
            pl.BlockSpec((tm, RP_W), row), prev_spec, pl.BlockSpec((1, RP_W), fix),
            vec_spec, vec_spec, vec_spec, vec_spec, vec_spec,
            pl.BlockSpec(w2.shape, fix), pl.BlockSpec(a2.shape, fix), pl.BlockSpec(g2.shape, fix),
            pl.BlockSpec(e.shape, fix), pl.BlockSpec(et.shape, fix),
        ],
        out_specs=[pl.BlockSpec((tm, C_RWKV), row)] * 8,
        out_shape=[out] * 8,
        compiler_params=_cparams(("parallel",)),
        name="rwkv_prep_seq" if sequence else "rwkv_prep_batch",
    )(proj, prev, mu, w0, a0, k_k, k_a, r_k, w2, a2, g2, e, et)


_NN = ((1,), (0,))
_NT = ((1,), (1,))
_TN = ((0,), (0,))


def _mm(a, b, dims):
    return lax.dot_general(a, b, (dims, ((), ())), preferred_element_type=F32)


def _rwkv_scan_kernel(r_ref, lw_ref, k_ref, v_ref, a_ref, b_ref, y_ref, s_out, s_scr):
    c = pl.program_id(0)
    ch = r_ref.shape[0]
    heads = range(H_RWKV)

    @pl.when(c == 0)
    def _():
        s_scr[...] = jnp.zeros_like(s_scr)

    row = lax.broadcasted_iota(jnp.int32, (ch, ch), 0)
    col = lax.broadcasted_iota(jnp.int32, (ch, ch), 1)
    incl = row >= col
    strict = row > col

    def cut(x):
        return [x[:, h * DH_RWKV:(h + 1) * DH_RWKV].astype(BF16) for h in heads]

    lw = lw_ref[...]
    cs = jnp.dot(jnp.where(incl, 1.0, 0.0), lw, precision=lax.Precision.HIGHEST, preferred_element_type=F32)
    p_incl = jnp.exp(cs)
    p_inv = jnp.exp(-cs)
    p_excl = jnp.exp(cs - lw)
    rm = cut(r_ref[...] * p_incl)
    am = cut(a_ref[...] * p_excl)
    bm = cut(b_ref[...] * p_inv)
    km = cut(k_ref[...] * p_inv)
    vm = cut(v_ref[...])
    p_end = p_incl[ch - 1:ch, :]
    s0 = [s_scr[h] for h in heads]
    s0b = [s.astype(BF16) for s in s0]

    def masked(xs, ys, keep):
        return [jnp.where(keep, _mm(x, y, _NT), 0.0).astype(BF16) for x, y in zip(xs, ys)]

    l_ab = masked(am, bm, strict)
    l_ak = masked(am, km, strict)
    m_rb = masked(rm, bm, incl)
    m_rk = masked(rm, km, incl)
    u = [_mm(am[h], s0b[h], _NT) + _mm(l_ak[h], vm[h], _NN) for h in heads]
    y = [_mm(rm[h], s0b[h], _NT) + _mm(m_rk[h], vm[h], _NN) for h in heads]
    pw = l_ab
    u = [u[h] + _mm(pw[h], u[h].astype(BF16), _NN) for h in heads]
    span = 2
    while span < ch:
        pw = [_mm(pw[h], pw[h], _NN).astype(BF16) for h in heads]
        u = [u[h] + _mm(pw[h], u[h].astype(BF16), _NN) for h in heads]
        span *= 2
    ub = [x.astype(BF16) for x in u]
    y = [y[h] + _mm(m_rb[h], ub[h], _NN) for h in heads]
    y_ref[...] = jnp.concatenate(y, axis=1)
    for h in heads:
        s_new = s0[h] + _mm(ub[h], bm[h], _TN) + _mm(vm[h], km[h], _TN)
        s_scr[h] = s_new * p_end[:, h * DH_RWKV:(h + 1) * DH_RWKV]

    @pl.when(c == pl.num_programs(0) - 1)
    def _():
        s_out[...] = s_scr[...]


def _rwkv_scan(r, lw, k, v, a, b):
    t = r.shape[0]
    spec = pl.BlockSpec((SCAN_CHUNK, C_RWKV), lambda c: (c, 0))
    st = (H_RWKV, DH_RWKV, DH_RWKV)
    return pl.pallas_call(
        _rwkv_scan_kernel,
        grid=(t // SCAN_CHUNK,),
        in_specs=[spec] * 6,
        out_specs=[spec, pl.BlockSpec(st, lambda c: (0, 0, 0))],
        out_shape=[jax.ShapeDtypeStruct((t, C_RWKV), F32), jax.ShapeDtypeStruct(st, F32)],
        scratch_shapes=[pltpu.VMEM(st, F32)],
        compiler_params=_cparams(("arbitrary",)),
        name="rwkv_scan",
    )(r, lw, k, v, a, b)


def _rwkv_step_kernel(r_ref, lw_ref, k_ref, v_ref, a_ref, b_ref, s_ref, y_ref, s_out):
    s = s_ref[...]
    dv, dk = s.shape[-2:]
    eye = (lax.broadcasted_iota(jnp.int32, (dv, dk), 0) == lax.broadcasted_iota(jnp.int32, (dv, dk), 1))
    sa = jnp.sum(s * a_ref[...], axis=-1, keepdims=True)
    v_col = jnp.sum(jnp.where(eye, v_ref[...], 0.0), axis=-1, keepdims=True)
    s_new = s * jnp.exp(lw_ref[...]) + sa * b_ref[...] + v_col * k_ref[...]
    y_col = jnp.sum(s_new * r_ref[...], axis=-1, keepdims=True)
    y_ref[...] = jnp.sum(jnp.where(eye, y_col, 0.0), axis=-2, keepdims=True)
    s_out[...] = s_new


def _rwkv_step(r, lw, k, v, a, b, state, *, nb):
    n = state.shape[0]
    vec = lambda z: z.reshape(n, H_RWKV, 1, DH_RWKV)
    vspec = pl.BlockSpec((nb, H_RWKV, 1, DH_RWKV), lambda i: (i, 0, 0, 0))
    sspec = pl.BlockSpec((nb, H_RWKV, DH_RWKV, DH_RWKV), lambda i: (i, 0, 0, 0))
    y, s_new = pl.pallas_call(
        _rwkv_step_kernel,
        grid=(n // nb,),
        in_specs=[vspec] * 6 + [sspec],
        out_specs=[vspec, sspec],
        out_shape=[jax.ShapeDtypeStruct((n, H_RWKV, 1, DH_RWKV), F32),
                   jax.ShapeDtypeStruct(state.shape, F32)],
        compiler_params=_cparams(("parallel",)),
        name="rwkv_step",
    )(vec(r), vec(lw), vec(k), vec(v), vec(a), vec(b), state)
    return y.reshape(n, C_RWKV), s_new


def _rwkv_post_kernel(y_ref, g_ref, bonus_ref, lng_ref, lnb_ref, e_ref, et_ref, o_ref):
    y = y_ref[...]
    inv_n = 1.0 / DH_RWKV
    mean = _seg_sum(y, e_ref, et_ref) * inv_n
    yc = y - mean
    var = _seg_sum(yc * yc, e_ref, et_ref) * inv_n
    yn = yc * lax.rsqrt(var + LN_X_EPS) * lng_ref[...] + lnb_ref[...]
    o_ref[...] = ((yn + bonus_ref[...]) * g_ref[...]).astype(o_ref.dtype)


def _rwkv_post(y, g, bonus, ln_g, ln_b, sel, *, tm):
    m = y.shape[0]
    e, et = sel
    row = pl.BlockSpec((tm, C_RWKV), lambda i: (i, 0))
    vec = pl.BlockSpec((1, C_RWKV), lambda i: (0, 0))
    return pl.pallas_call(
        _rwkv_post_kernel,
        grid=(m // tm,),
        in_specs=[row, row, row, vec, vec,
                  pl.BlockSpec(e.shape, lambda i: (0, 0)), pl.BlockSpec(et.shape, lambda i: (0, 0))],
        out_specs=row,
        out_shape=jax.ShapeDtypeStruct((m, C_RWKV), BF16),
        compiler_params=_cparams(("parallel",)),
        name="rwkv_post",
    )(y, g, bonus, ln_g, ln_b, e, et)


def _pad_cols(w, width):
    return jnp.pad(w, ((0, 0), (0, width - w.shape[1])))


def _pad_rows(w, height):
    return jnp.pad(w, ((0, height - w.shape[0]), (0, 0)))


def _repack_rwkv_cols(w):
    wl0 = 3 * C_RWKV
    al0 = wl0 + W_LORA
    gl0 = al0 + A_LORA
    return jnp.concatenate([
        w[:, :wl0],
        _pad_cols(w[:, wl0:al0], LANES),
        _pad_cols(w[:, al0:gl0], LANES),
        _pad_cols(w[:, gl0:], 2 * LANES),
    ], axis=1)


def _unpack_rwkv_cols(w):
    return jnp.concatenate([
        w[:, :RP_WL], w[:, RP_WL:RP_WL + W_LORA], w[:, RP_AL:RP_AL + A_LORA], w[:, RP_GL:RP_GL + G_LORA],
    ], axis=1)


def _head_selectors():
    lane_head = jnp.arange(C_RWKV) // DH_RWKV
    e = (lane_head[:, None] == jnp.arange(LANES)[None, :]).astype(BF16)
    return e, e.T


def kernel(x_prompt, x_sample, cache_k, cache_v, cache_logf, state_wkv, state_shift, page_table,
           p_prompt, p_sample,
           ffn1_norm, ffn1_w_gate, ffn1_w_up, ffn1_w_down, mix_norm, w_in, fox_b_f,
           rwkv_mu, rwkv_w0, rwkv_w2, rwkv_a0, rwkv_a2, rwkv_g2, rwkv_k_k, rwkv_k_a, rwkv_r_k,
           rwkv_ln_g, rwkv_ln_b, w_o_fox, w_o_rwkv, w_out,
           ffn2_norm, ffn2_w_gate, ffn2_w_up, ffn2_w_down, ple_norm, ple_w_gate, ple_w_proj,
           final_norm):
    depth = w_in.shape[0]
    n_pr, s_len, d_model = x_prompt.shape
    n_dec = x_sample.shape[0]
    assert n_pr == 1 and x_sample.shape[1] == 1
    m_rows = s_len + n_dec
    n_pages = page_table.shape[1]
    pps = DECODE_PAGES_PER_STEP if n_pages % DECODE_PAGES_PER_STEP == 0 else 1
    tm = 640 if m_rows % 640 == 0 else LANES
    tm_wide = 1664 if m_rows % 1664 == 0 else tm
    tq = min(512, s_len)
    t_prep = min(256, s_len)

    off_f = 3 * C_FOX
    off_r = off_f + H_FOX
    off_g = off_r + N_RWKV
    qkv_scale = jnp.concatenate([jnp.full((1, C_FOX), LOG2E * DH_FOX ** -0.5, F32), jnp.ones((1, 2 * C_FOX), F32)], axis=1)
    sel = _head_selectors()
    row2 = lambda z: z.reshape(1, -1).astype(F32)

    x = jnp.concatenate([x_prompt[0], x_sample[:, 0]], axis=0)
    outs = {name: [] for name in ("kp", "vp", "lfp", "wkvp", "shp", "ks", "vs", "lfs", "wkvs", "shs")}
    for i in range(depth):
        bf = lambda w: w[i].astype(BF16)
        w_i = w_in[i]
        w_qkv = w_i[:, :off_f].astype(BF16)
        w_f = _pad_cols(w_i[:, off_f:off_r], LANES).astype(BF16)
        w_r = _repack_rwkv_cols(w_i[:, off_r:off_g]).astype(BF16)
        w_g = w_i[:, off_g:].astype(BF16)
        b_f = _pad_cols(row2(fox_b_f[i]), LANES)

        x, h = _ffn(x, row2(ffn1_norm[i]), row2(mix_norm[i]),
                    bf(ffn1_w_gate), bf(ffn1_w_up), bf(ffn1_w_down), tm=tm, tf=512)

        qkv, qkv_bf = _proj(h, w_qkv, qkv_scale, mode="qkv", tm=tm_wide, tn=512)
        logf = _proj(h, w_f, b_f, mode="logf", tm=tm_wide, tn=LANES)[0][:, :H_FOX]
        proj_r = _proj(h, w_r, jnp.zeros((1, RP_W), F32), mode="plain", tm=tm_wide, tn=512)[0]
        gates = _proj(h, w_g, jnp.zeros((1, 2 * d_model), F32), mode="sigmoid", tm=tm_wide, tn=1024)[0]

        lf_p = logf[:s_len]
        c_row = _cumsum_rows(lf_p.T)
        o_fox_p = _fox_prompt(qkv_bf, c_row.T, s_len=s_len, tq=tq, tk=tq)
        q_s = qkv[s_len:, :C_FOX].reshape(n_dec, H_FOX, DH_FOX)
        k_s = qkv[s_len:, C_FOX:2 * C_FOX].reshape(n_dec, H_FOX, DH_FOX)
        v_s = qkv[s_len:, 2 * C_FOX:].reshape(n_dec, H_FOX, DH_FOX)
        lf_s = logf[s_len:]
        o_fox_s = _fox_decode(
            i, page_table, q_s, k_s, v_s, jnp.tile(lf_s, (1, LANES // H_FOX)).reshape(n_dec, 1, LANES),
            cache_k, cache_v, cache_logf, pps=pps)
        o_fox = jnp.concatenate([o_fox_p, o_fox_s.reshape(n_dec, C_FOX).astype(BF16)], axis=0)

        vecs = (_repack_rwkv_cols(row2(rwkv_mu[i])), row2(rwkv_w0[i]), row2(rwkv_a0[i]),
                row2(rwkv_k_k[i]), row2(rwkv_k_a[i]), row2(rwkv_r_k[i]))
        mats = (_pad_rows(rwkv_w2[i], LANES).astype(BF16), _pad_rows(rwkv_a2[i], LANES).astype(BF16),
                _pad_rows(rwkv_g2[i], 2 * LANES).astype(BF16))
        proj_rp, proj_rs = proj_r[:s_len], proj_r[s_len:]
        pre_p = _rwkv_prep(proj_rp, proj_rp, vecs, mats, sel, sequence=True, tm=t_prep)
        prev_s = _repack_rwkv_cols(state_shift[i].astype(F32))
        pre_s = _rwkv_prep(proj_rs, prev_s, vecs, mats, sel, sequence=False, tm=n_dec)
        y_p, wkv_p = _rwkv_scan(*pre_p[:6])
        y_s, wkv_s = _rwkv_step(*pre_s[:6], state_wkv[i].astype(F32), nb=8)
        cat = lambda a, b: jnp.concatenate([a, b], axis=0)
        o_rwkv = _rwkv_post(cat(y_p, y_s), cat(pre_p[6], pre_s[6]), cat(pre_p[7], pre_s[7]),
                            row2(rwkv_ln_g[i]), row2(rwkv_ln_b[i]), sel, tm=tm)

        x = _merge(x, o_fox, o_rwkv, gates, bf(w_o_fox), bf(w_o_rwkv), bf(w_out), tm=tm, tn=512)
        x, u = _ffn(x, row2(ffn2_norm[i]), row2(ple_norm[i]),
                    bf(ffn2_w_gate), bf(ffn2_w_up), bf(ffn2_w_down), tm=tm, tf=512)
        p_all = jnp.concatenate([p_prompt[i, 0], p_sample[i, :, 0]], axis=0)
        x = _ple(x, u, p_all, bf(ple_w_gate), bf(ple_w_proj), tm=tm_wide, tn=512)

        outs["kp"].append(qkv[:s_len, C_FOX:2 * C_FOX].reshape(1, s_len, H_FOX, DH_FOX))
        outs["vp"].append(qkv[:s_len, 2 * C_FOX:].reshape(1, s_len, H_FOX, DH_FOX))
        outs["lfp"].append(lf_p.reshape(1, s_len, H_FOX))
        outs["wkvp"].append(wkv_p[None])
        outs["shp"].append(_unpack_rwkv_cols(proj_rp[s_len - 1:s_len]))
        outs["ks"].append(k_s.reshape(n_dec, 1, H_FOX, DH_FOX))
        outs["vs"].append(v_s.reshape(n_dec, 1, H_FOX, DH_FOX))
        outs["lfs"].append(lf_s.reshape(n_dec, 1, H_FOX))
        outs["wkvs"].append(wkv_s)
        outs["shs"].append(_unpack_rwkv_cols(proj_rs))

    y = _final_norm(x, row2(final_norm), tm=tm)
    st = {k: jnp.stack(v) for k, v in outs.items()}
    return (y[:s_len].reshape(1, s_len, d_model), y[s_len:].reshape(n_dec, 1, d_model),
            st["kp"], st["vp"], st["lfp"], st["wkvp"], st["shp"],
            st["ks"], st["vs"], st["lfs"], st["wkvs"], st["shs"])
```

```python
import functools

import jax
import jax.numpy as jnp
from jax import lax
from jax.experimental import pallas as pl
from jax.experimental.pallas import tpu as pltpu

F32 = jnp.float32
BF16 = jnp.bfloat16

NORM_EPS = 1e-6
LN_X_EPS = 64e-5
KK_EPS = 1e-12

LANES = 128
VMEM_LIMIT = 56 * 1024 * 1024

H_FOX = 8
DH_FOX = 128
C_FOX = H_FOX * DH_FOX
H_RWKV = 16
DH_RWKV = 64
C_RWKV = H_RWKV * DH_RWKV
W_LORA = 64
A_LORA = 64
G_LORA = 160
N_RWKV = 3 * C_RWKV + W_LORA + A_LORA + G_LORA
RP_WL = 3 * C_RWKV
RP_AL = RP_WL + LANES
RP_GL = RP_AL + LANES
RP_W = RP_GL + 2 * LANES
LOG2E = 1.4426950408889634
DECODE_PAGES_PER_STEP = 8
SCAN_CHUNK = 64


def _cparams(sem):
    return pltpu.CompilerParams(dimension_semantics=sem, vmem_limit_bytes=VMEM_LIMIT)


def _sigmoid(x):
    return 1.0 / (1.0 + jnp.exp(-x))


def _softplus(x):
    return jnp.maximum(x, 0.0) + jnp.log(1.0 + jnp.exp(-jnp.abs(x)))


def _rms_rows(x, g):
    ms = jnp.mean(x * x, axis=-1, keepdims=True)
    return x * lax.rsqrt(ms + NORM_EPS) * g


def _dot(a, b):
    return jnp.dot(a, b, preferred_element_type=F32)


def _split2(x):
    hi = x.astype(BF16)
    lo = (x - hi.astype(F32)).astype(BF16)
    return hi, lo


def _ffn_kernel(x_ref, g_ref, gn_ref, wg_ref, wu_ref, wd_ref, o_ref, hn_ref, h_scr, acc_scr):
    f = pl.program_id(1)

    @pl.when(f == 0)
    def _():
        h_scr[...] = _rms_rows(x_ref[...], g_ref[...]).astype(BF16)
        acc_scr[...] = jnp.zeros_like(acc_scr)

    h = h_scr[...]
    gate = _dot(h, wg_ref[...])
    up = _dot(h, wu_ref[...])
    act = (gate * _sigmoid(gate) * up).astype(BF16)
    acc_scr[...] += _dot(act, wd_ref[...])

    @pl.when(f == pl.num_programs(1) - 1)
    def _():
        out = x_ref[...] + 0.5 * acc_scr[...]
        o_ref[...] = out
        hn_ref[...] = _rms_rows(out, gn_ref[...]).astype(BF16)


def _cast_kernel(w_ref, o_ref):
    o_ref[...] = w_ref[...].astype(o_ref.dtype)


def _to_bf16(w):
    depth, rows, cols = w.shape
    tr = next(t for t in (512, 256, 128, 64, 32, 16) if rows % t == 0 and t * cols * 4 <= 6 * 2 ** 20)
    spec = pl.BlockSpec((None, tr, cols), lambda l, r: (l, r, 0))
    return pl.pallas_call(
        _cast_kernel,
        grid=(depth, rows // tr),
        in_specs=[spec],
        out_specs=spec,
        out_shape=jax.ShapeDtypeStruct(w.shape, BF16),
        compiler_params=_cparams(("parallel", "parallel")),
        name="to_bf16",
    )(w)


def _ffn(layer, x, g, g_next, wg, wu, wd, *, tm, tf):
    m, d = x.shape
    dff = wg.shape[2]
    row = pl.BlockSpec((tm, d), lambda i, f: (i, 0))
    vec = pl.BlockSpec((1, d), lambda i, f: (0, 0))
    return pl.pallas_call(
        _ffn_kernel,
        grid=(m // tm, dff // tf),
        in_specs=[
            row, vec, vec,
            pl.BlockSpec((None, d, tf), lambda i, f: (layer, 0, f)),
            pl.BlockSpec((None, d, tf), lambda i, f: (layer, 0, f)),
            pl.BlockSpec((None, tf, d), lambda i, f: (layer, f, 0)),
        ],
        out_specs=[row, row],
        out_shape=[jax.ShapeDtypeStruct((m, d), F32), jax.ShapeDtypeStruct((m, d), BF16)],
        scratch_shapes=[pltpu.VMEM((tm, d), BF16), pltpu.VMEM((tm, d), F32)],
        compiler_params=_cparams(("parallel", "arbitrary")),
        name="ffn",
    )(x, g, g_next, wg, wu, wd)


def _proj_kernel(h_ref, w_ref, aux_ref, *outs, mode):
    acc = _dot(h_ref[...], w_ref[...])
    if mode == "qkv":
        outs[0][...] = acc
        outs[1][...] = (acc * aux_ref[...]).astype(BF16)
    elif mode == "logf":
        outs[0][...] = -_softplus(-(acc + aux_ref[...]))
    elif mode == "sigmoid":
        outs[0][...] = _sigmoid(acc)
    else:
        outs[0][...] = acc


def _proj(h, w, aux, *, mode, tm, tn):
    m, d = h.shape
    n = w.shape[1]
    out_shape = [jax.ShapeDtypeStruct((m, n), F32)]
    if mode == "qkv":
        out_shape.append(jax.ShapeDtypeStruct((m, n), BF16))
    o_spec = pl.BlockSpec((tm, tn), lambda i, j: (i, j))
    return pl.pallas_call(
        functools.partial(_proj_kernel, mode=mode),
        grid=(m // tm, n // tn),
        in_specs=[
            pl.BlockSpec((tm, d), lambda i, j: (i, 0)),
            pl.BlockSpec((d, tn), lambda i, j: (0, j)),
            pl.BlockSpec((1, tn), lambda i, j: (0, j)),
        ],
        out_specs=[o_spec] * len(out_shape),
        out_shape=out_shape,
        compiler_params=_cparams(("parallel", "arbitrary")),
        name="proj_" + mode,
    )(h, w, aux)


def _merge_kernel(x_ref, of_ref, or_ref, gf_ref, gr_ref, wof_ref, wor_ref, wout_ref, o_ref, acc_scr):
    n = pl.program_id(1)

    @pl.when(n == 0)
    def _():
        acc_scr[...] = jnp.zeros_like(acc_scr)

    merged = gf_ref[...] * _dot(of_ref[...], wof_ref[...]) + gr_ref[...] * _dot(or_ref[...], wor_ref[...])
    acc_scr[...] += _dot(merged.astype(BF16), wout_ref[...])

    @pl.when(n == pl.num_programs(1) - 1)
    def _():
        o_ref[...] = x_ref[...] + acc_scr[...]


def _merge(layer, x, o_fox, o_rwkv, gates, wof, wor, wout, *, tm, tn):
    m, d = x.shape
    nb = d // tn
    return pl.pallas_call(
        _merge_kernel,
        grid=(m // tm, nb),
        in_specs=[
            pl.BlockSpec((tm, d), lambda i, n: (i, 0)),
            pl.BlockSpec((tm, o_fox.shape[1]), lambda i, n: (i, 0)),
            pl.BlockSpec((tm, o_rwkv.shape[1]), lambda i, n: (i, 0)),
            pl.BlockSpec((tm, tn), lambda i, n: (i, n)),
            pl.BlockSpec((tm, tn), lambda i, n: (i, n + nb)),
            pl.BlockSpec((None, wof.shape[1], tn), lambda i, n: (layer, 0, n)),
            pl.BlockSpec((None, wor.shape[1], tn), lambda i, n: (layer, 0, n)),
            pl.BlockSpec((None, tn, d), lambda i, n: (layer, n, 0)),
        ],
        out_specs=pl.BlockSpec((tm, d), lambda i, n: (i, 0)),
        out_shape=jax.ShapeDtypeStruct((m, d), F32),
        scratch_shapes=[pltpu.VMEM((tm, d), F32)],
        compiler_params=_cparams(("parallel", "arbitrary")),
        name="merge",
    )(x, o_fox, o_rwkv, gates, gates, wof, wor, wout)


def _ple_kernel(x_ref, u_ref, p_ref, wg_ref, wp_ref, o_ref):
    gate = _sigmoid(_dot(u_ref[...], wg_ref[...]))
    emb = _dot(p_ref[...].astype(BF16), wp_ref[...])
    o_ref[...] = x_ref[...] + gate * emb


def _ple(layer, x, u, p, wg, wp, *, tm, tn):
    m, d = x.shape
    return pl.pallas_call(
        _ple_kernel,
        grid=(m // tm, d // tn),
        in_specs=[
            pl.BlockSpec((tm, tn), lambda i, j: (i, j)),
            pl.BlockSpec((tm, d), lambda i, j: (i, 0)),
            pl.BlockSpec((tm, p.shape[1]), lambda i, j: (i, 0)),
            pl.BlockSpec((None, d, tn), lambda i, j: (layer, 0, j)),
            pl.BlockSpec((None, p.shape[1], tn), lambda i, j: (layer, 0, j)),
        ],
        out_specs=pl.BlockSpec((tm, tn), lambda i, j: (i, j)),
        out_shape=jax.ShapeDtypeStruct((m, d), F32),
        compiler_params=_cparams(("parallel", "arbitrary")),
        name="ple",
    )(x, u, p, wg, wp)


def _norm_kernel(x_ref, g_ref, o_ref):
    o_ref[...] = _rms_rows(x_ref[...], g_ref[...])


def _final_norm(x, g, *, tm):
    m, d = x.shape
    return pl.pallas_call(
        _norm_kernel,
        grid=(m // tm,),
        in_specs=[pl.BlockSpec((tm, d), lambda i: (i, 0)), pl.BlockSpec((1, d), lambda i: (0, 0))],
        out_specs=pl.BlockSpec((tm, d), lambda i: (i, 0)),
        out_shape=jax.ShapeDtypeStruct((m, d), F32),
        compiler_params=_cparams(("parallel",)),
        name="final_norm",
    )(x, g)


def _cumsum_kernel(lf_ref, o_ref):
    h, s = lf_ref.shape
    row = lax.broadcasted_iota(jnp.int32, (LANES, LANES), 0)
    col = lax.broadcasted_iota(jnp.int32, (LANES, LANES), 1)
    tri = jnp.where(row <= col, 1.0, 0.0)

    def body(c, carry):
        off = pl.multiple_of(c * LANES, LANES)
        y = jnp.dot(lf_ref[:, pl.ds(off, LANES)], tri, precision=lax.Precision.HIGHEST,
                    preferred_element_type=F32) + carry
        o_ref[:, pl.ds(off, LANES)] = y
        return y[:, LANES - 1:LANES]

    lax.fori_loop(0, s // LANES, body, jnp.zeros((h, 1), F32))


def _cumsum_rows(lf_t):
    return pl.pallas_call(
        _cumsum_kernel,
        out_shape=jax.ShapeDtypeStruct(lf_t.shape, F32),
        compiler_params=pltpu.CompilerParams(vmem_limit_bytes=VMEM_LIMIT),
        name="logf_cumsum",
    )(lf_t)


def _fox_prompt_kernel(q_ref, k_ref, v_ref, qx_ref, kx_ref, o_ref, m_scr, l_scr, acc_scr, *, tq, tk):
    qi = pl.program_id(0)
    kj = pl.program_id(1)

    @pl.when(kj == 0)
    def _():
        m_scr[...] = jnp.full_like(m_scr, -jnp.inf)
        l_scr[...] = jnp.zeros_like(l_scr)
        acc_scr[...] = jnp.zeros_like(acc_scr)

    def update(on_diagonal):
        if on_diagonal:
            causal = (lax.broadcasted_iota(jnp.int32, (tq, tk), 1)
                      <= lax.broadcasted_iota(jnp.int32, (tq, tk), 0))
        heads = range(H_FOX)
        sl = [slice(h * DH_FOX, (h + 1) * DH_FOX) for h in heads]
        ones = jnp.ones((tk, LANES), BF16)
        t = []
        for h in heads:
            q_wide = jnp.concatenate([q_ref[:, sl[h]], qx_ref[h]], axis=1)
            k_wide = jnp.concatenate([k_ref[:, sl[h]], kx_ref[h]], axis=1)
            th = lax.dot_general(q_wide, k_wide, (_NT, ((), ())), preferred_element_type=F32)
            t.append(jnp.where(causal, th, -jnp.inf) if on_diagonal else th)
        m_old = [m_scr[h] for h in heads]
        m_new = [jnp.maximum(m_old[h], jnp.max(t[h], axis=-1, keepdims=True)) for h in heads]
        alpha = [jnp.exp2(m_old[h] - m_new[h]) for h in heads]
        p = [jnp.exp2(t[h] - jnp.tile(m_new[h], (1, tk // LANES))).astype(BF16) for h in heads]
        for h in heads:
            pv = _dot(p[h], jnp.concatenate([v_ref[:, sl[h]], ones], axis=1))
            acc_scr[h] = alpha[h] * acc_scr[h] + pv[:, :DH_FOX]
            l_scr[h] = alpha[h] * l_scr[h] + pv[:, DH_FOX:]
            m_scr[h] = m_new[h]

    @pl.when(kj < qi)
    def _():
        update(False)

    @pl.when(kj == qi)
    def _():
        update(True)
        for h in range(H_FOX):
            o_ref[:, h * DH_FOX:(h + 1) * DH_FOX] = (acc_scr[h] / l_scr[h]).astype(o_ref.dtype)


def _forget_blocks(c_col):
    def top16(x):
        bits = lax.bitcast_convert_type(x, jnp.uint32) & jnp.uint32(0xFFFF0000)
        return lax.bitcast_convert_type(bits, F32)

    c2 = (c_col * LOG2E).T[:, :, None]
    hi = top16(c2)
    mid = top16(c2 - hi)
    lo = c2 - hi - mid
    parts = jnp.concatenate([hi, mid, lo], axis=2)
    ones = jnp.ones_like(parts)
    pad = jnp.zeros(parts.shape[:2] + (LANES - 6,), F32)
    qx = jnp.concatenate([parts, ones, pad], axis=2).astype(BF16)
    kx = jnp.concatenate([ones, -parts, pad], axis=2).astype(BF16)
    return qx, kx


def _fox_prompt(qkv_bf, c_col, *, s_len, tq, tk):
    assert tq == tk and DH_FOX == LANES
    nq, nk = s_len // tq, s_len // tk
    qx, kx = _forget_blocks(c_col)
    return pl.pallas_call(
        functools.partial(_fox_prompt_kernel, tq=tq, tk=tk),
        grid=(nq, nk),
        in_specs=[
            pl.BlockSpec((tq, C_FOX), lambda i, j: (i, 0)),
            pl.BlockSpec((tk, C_FOX), lambda i, j: (jnp.minimum(i, j), 1)),
            pl.BlockSpec((tk, C_FOX), lambda i, j: (jnp.minimum(i, j), 2)),
            pl.BlockSpec((H_FOX, tq, LANES), lambda i, j: (0, i, 0)),
            pl.BlockSpec((H_FOX, tk, LANES), lambda i, j: (0, jnp.minimum(i, j), 0)),
        ],
        out_specs=pl.BlockSpec((tq, C_FOX), lambda i, j: (i, 0)),
        out_shape=jax.ShapeDtypeStruct((s_len, C_FOX), BF16),
        scratch_shapes=[
            pltpu.VMEM((H_FOX, tq, LANES), F32),
            pltpu.VMEM((H_FOX, tq, LANES), F32),
            pltpu.VMEM((H_FOX, tq, DH_FOX), F32),
        ],
        compiler_params=_cparams(("parallel", "arbitrary")),
        name="fox_prompt",
    )(qkv_bf, qkv_bf, qkv_bf, qx, kx)


def _fox_decode_kernel(pt_ref, q_ref, kn_ref, vn_ref, lfn_ref, sfx_ref, *rest, scale, pps):
    del pt_ref
    kc, vc, lfc = rest[:pps], rest[pps:2 * pps], rest[2 * pps:3 * pps]
    o_ref, m_scr, l_scr, acc_scr, carry_scr = rest[3 * pps:]
    p = pl.program_id(1)
    page = kc[0].shape[0]
    nrow = page * H_FOX
    sub = lax.broadcasted_iota(jnp.int32, (H_FOX, nrow + LANES), 0)
    lane = lax.broadcasted_iota(jnp.int32, (H_FOX, nrow + LANES), 1)
    own_ext = (lane % H_FOX) == sub
    own = own_ext[:, :nrow]
    pick = lane[:, :LANES] == sub[:, :LANES]
    q8 = q_ref[...] * scale

    def to_col(row):
        return jnp.sum(jnp.where(pick, row, 0.0), axis=-1, keepdims=True)

    def per_head(row, op):
        parts = [row[:, j * LANES:(j + 1) * LANES] for j in range(nrow // LANES)]
        while len(parts) > 1:
            parts = [op(parts[i], parts[i + 1]) for i in range(0, len(parts), 2)]
        acc = parts[0]
        shift = H_FOX
        while shift < LANES:
            acc = op(acc, pltpu.roll(acc, shift, axis=1))
            shift *= 2
        return acc

    @pl.when(p == 0)
    def _():
        m_scr[...] = jnp.full_like(m_scr, -jnp.inf)
        l_scr[...] = jnp.zeros_like(l_scr)
        acc_scr[...] = jnp.zeros_like(acc_scr)
        carry_scr[...] = lfn_ref[...]

    q8_bf = q8.astype(BF16)
    widen = lambda row: jnp.tile(row, (1, nrow // LANES))
    carry = carry_scr[...]
    scores = []
    for j in range(pps):
        k2 = kc[j][...].reshape(nrow, DH_FOX).astype(BF16)
        g = lax.dot_general(q8_bf, k2, (_NT, ((), ())), preferred_element_type=F32)
        s = jnp.sum(jnp.where(own, g, 0.0), axis=0, keepdims=True)

        lf = lfc[j][...]
        hi = lf.astype(BF16).astype(F32)
        lf2 = jnp.concatenate([hi, lf - hi], axis=0).astype(BF16)
        t = lax.dot_general(lf2, sfx_ref[...], (_TN, ((), ())), preferred_element_type=F32)
        bias = jnp.sum(jnp.where(own_ext, t, 0.0), axis=0, keepdims=True)
        scores.append(s + bias[:, :nrow] + widen(carry))
        carry = carry + bias[:, nrow:]
    carry_scr[...] = carry

    m_old = m_scr[...]
    m_new = m_old
    for s in scores:
        m_new = jnp.maximum(m_new, per_head(s, jnp.maximum))
    alpha = jnp.exp(m_old - m_new)
    m_wide = widen(m_new)
    l_new = alpha * l_scr[...]
    acc = to_col(alpha) * acc_scr[...]
    for j, s in enumerate(scores):
        pr = jnp.exp(s - m_wide)
        l_new = l_new + per_head(pr, jnp.add)
        p8 = jnp.where(own, pr, 0.0).astype(BF16)
        acc = acc + _dot(p8, vc[j][...].reshape(nrow, DH_FOX).astype(BF16))
    l_scr[...] = l_new
    acc_scr[...] = acc
    m_scr[...] = m_new

    @pl.when(p == pl.num_programs(1) - 1)
    def _():
        s_new = jnp.sum(q8 * kn_ref[...], axis=-1, keepdims=True)
        m_old = to_col(m_scr[...])
        m_fin = jnp.maximum(m_old, s_new)
        alpha = jnp.exp(m_old - m_fin)
        p_new = jnp.exp(s_new - m_fin)
        l_fin = alpha * to_col(l_scr[...]) + p_new
        o_ref[...] = (alpha * acc_scr[...] + p_new * vn_ref[...]) / l_fin


def _suffix_selector(page):
    r = jnp.arange(page * H_FOX + LANES)
    tok = jnp.arange(page)
    later = jnp.where(r[None, :] < page * H_FOX, tok[:, None] > r[None, :] // H_FOX, True)
    return jnp.tile(later.astype(BF16), (2, 1))


def _fox_decode(layer, page_table, q, k_new, v_new, lf_new, cache_k, cache_v, cache_lf, *, pps):
    n, n_pages = page_table.shape
    page = cache_k.shape[2]
    pt_flat = page_table.reshape(-1)
    sfx = _suffix_selector(page)

    def page_map(j, tail):
        def index_map(b, p, pt):
            return (layer, pt[b * n_pages + (n_pages - 1 - (p * pps + j))]) + tail
        return index_map

    row_spec = pl.BlockSpec((None, H_FOX, DH_FOX), lambda b, p, pt: (b, 0, 0))
    kv_specs = [pl.BlockSpec((None, None, page, H_FOX, DH_FOX), page_map(j, (0, 0, 0))) for j in range(pps)]
    lf_specs = [pl.BlockSpec((None, None, page, H_FOX), page_map(j, (0, 0))) for j in range(pps)]
    grid_spec = pltpu.PrefetchScalarGridSpec(
        num_scalar_prefetch=1,
        grid=(n, n_pages // pps),
        in_specs=[
            row_spec, row_spec, row_spec,
            pl.BlockSpec((None, 1, LANES), lambda b, p, pt: (b, 0, 0)),
            pl.BlockSpec(sfx.shape, lambda b, p, pt: (0, 0)),
        ] + kv_specs + kv_specs + lf_specs,
        out_specs=row_spec,
        scratch_shapes=[
            pltpu.VMEM((1, LANES), F32),
            pltpu.VMEM((1, LANES), F32),
            pltpu.VMEM((H_FOX, DH_FOX), F32),
            pltpu.VMEM((1, LANES), F32),
        ],
    )
    return pl.pallas_call(
        functools.partial(_fox_decode_kernel, scale=DH_FOX ** -0.5, pps=pps),
        grid_spec=grid_spec,
        out_shape=jax.ShapeDtypeStruct((n, H_FOX, DH_FOX), F32),
        compiler_params=_cparams(("parallel", "arbitrary")),
        name="fox_decode",
    )(pt_flat, q, k_new, v_new, lf_new, sfx, *([cache_k] * pps), *([cache_v] * pps), *([cache_lf] * pps))


def _seg_sum(x, e_ref, et_ref):
    hi, lo = _split2(x)
    s = _dot(hi, e_ref[...]) + _dot(lo, e_ref[...])
    shi, slo = _split2(s)
    return _dot(shi, et_ref[...]) + _dot(slo, et_ref[...])


def _rwkv_prep_kernel(pr_ref, pv_ref, ps_ref, mu_ref, w0_ref, a0_ref, kk_ref, ka_ref, rk_ref,
                      w2_ref, a2_ref, g2_ref, e_ref, et_ref,
                      r_out, lw_out, k_out, v_out, a_out, b_out, g_out, bonus_out, *, seq_tiles):
    i = pl.program_id(0)
    pr = pr_ref[...]
    first = jnp.where(i == 0, 0.0, pv_ref[7:8, :])
    prev = jnp.where(lax.broadcasted_iota(jnp.int32, pr.shape, 0) == 0, first, pltpu.roll(pr, 1, axis=0))
    prev = jnp.where(i < seq_tiles, prev, ps_ref[...])
    xm = pr + (prev - pr) * mu_ref[...]
    r = xm[:, 0:C_RWKV]
    k = xm[:, C_RWKV:2 * C_RWKV]
    v = xm[:, 2 * C_RWKV:3 * C_RWKV]
    wl = xm[:, RP_WL:RP_AL]
    al = xm[:, RP_AL:RP_GL]
    gl = xm[:, RP_GL:RP_W]

    w_log = -_softplus(-(w0_ref[...] + _dot(jnp.tanh(wl).astype(BF16), w2_ref[...]))) - 0.5
    a_sig = _sigmoid(a0_ref[...] + _dot(al.astype(BF16), a2_ref[...]))
    g = _dot(_sigmoid(gl).astype(BF16), g2_ref[...])

    kk = k * kk_ref[...]
    norm = jnp.sqrt(_seg_sum(kk * kk, e_ref, et_ref))
    kk = kk / jnp.maximum(norm, KK_EPS)
    k_mod = k * (1.0 + (a_sig - 1.0) * ka_ref[...])

    r_out[...] = r
    lw_out[...] = -jnp.exp(w_log)
    k_out[...] = k_mod
    v_out[...] = v
    a_out[...] = -kk
    b_out[...] = kk * a_sig
    g_out[...] = g
    bonus_out[...] = _seg_sum(r * k_mod * rk_ref[...], e_ref, et_ref) * v


def _rwkv_prep(proj, prev_rows, vecs, mats, sel, *, s_len, tm):
    m = proj.shape[0]
    assert s_len % tm == 0 and (m - s_len) % tm == 0
    seq_tiles = s_len // tm
    mu, w0, a0, k_k, k_a, r_k = vecs
    w2, a2, g2 = mats
    e, et = sel
    row = lambda i: (i, 0)
    fix = lambda i: (0, 0)
    vec_spec = pl.BlockSpec((1, C_RWKV), fix)
    out = jax.ShapeDtypeStruct((m, C_RWKV), F32)
    return pl.pallas_call(
        functools.partial(_rwkv_prep_kernel, seq_tiles=seq_tiles),
        grid=(m // tm,),
        in_specs=[
            pl.BlockSpec((tm, RP_W), row),
            pl.BlockSpec((8, RP_W), lambda i: (jnp.maximum(i * (tm // 8) - 1, 0), 0)),
            pl.BlockSpec((tm, RP_W), lambda i: (jnp.maximum(i - seq_tiles, 0), 0)),
            pl.BlockSpec((1, RP_W), fix),
            vec_spec, vec_spec, vec_spec, vec_spec, vec_spec,
            pl.BlockSpec(w2.shape, fix), pl.BlockSpec(a2.shape, fix), pl.BlockSpec(g2.shape, fix),
            pl.BlockSpec(e.shape, fix), pl.BlockSpec(et.shape, fix),
        ],
        out_specs=[pl.BlockSpec((tm, C_RWKV), row)] * 8,
        out_shape=[out] * 8,
        compiler_params=_cparams(("parallel",)),
        name="rwkv_prep",
    )(proj, proj, prev_rows, mu, w0, a0, k_k, k_a, r_k, w2, a2, g2, e, et)


_NN = ((1,), (0,))
_NT = ((1,), (1,))
_TN = ((0,), (0,))


def _mm(a, b, dims):
    return lax.dot_general(a, b, (dims, ((), ())), preferred_element_type=F32)


def _rwkv_scan_kernel(r_ref, lw_ref, k_ref, v_ref, a_ref, b_ref, y_ref, s_out, s_scr):
    c = pl.program_id(0)
    ch = r_ref.shape[0]
    heads = range(H_RWKV)

    @pl.when(c == 0)
    def _():
        s_scr[...] = jnp.zeros_like(s_scr)

    row = lax.broadcasted_iota(jnp.int32, (ch, ch), 0)
    col = lax.broadcasted_iota(jnp.int32, (ch, ch), 1)
    incl = row >= col
    strict = row > col

    def cut(x):
        return [x[:, h * DH_RWKV:(h + 1) * DH_RWKV].astype(BF16) for h in heads]

    lw = lw_ref[...]
    cs = jnp.dot(jnp.where(incl, 1.0, 0.0), lw, precision=lax.Precision.HIGHEST, preferred_element_type=F32)
    p_incl = jnp.exp(cs)
    p_inv = jnp.exp(-cs)
    p_excl = jnp.exp(cs - lw)
    rm = cut(r_ref[...] * p_incl)
    am = cut(a_ref[...] * p_excl)
    bm = cut(b_ref[...] * p_inv)
    km = cut(k_ref[...] * p_inv)
    vm = cut(v_ref[...])
    p_end = p_incl[ch - 1:ch, :]
    s0 = [s_scr[h] for h in heads]
    s0b = [s.astype(BF16) for s in s0]

    def masked(xs, ys, keep):
        return [jnp.where(keep, _mm(x, y, _NT), 0.0).astype(BF16) for x, y in zip(xs, ys)]

    l_ab = masked(am, bm, strict)
    l_ak = masked(am, km, strict)
    m_rb = masked(rm, bm, incl)
    m_rk = masked(rm, km, incl)
    u = [_mm(am[h], s0b[h], _NT) + _mm(l_ak[h], vm[h], _NN) for h in heads]
    y = [_mm(rm[h], s0b[h], _NT) + _mm(m_rk[h], vm[h], _NN) for h in heads]
    pw = l_ab
    u = [u[h] + _mm(pw[h], u[h].astype(BF16), _NN) for h in heads]
    span = 2
    while span < ch:
        pw = [_mm(pw[h], pw[h], _NN).astype(BF16) for h in heads]
        u = [u[h] + _mm(pw[h], u[h].astype(BF16), _NN) for h in heads]
        span *= 2
    ub = [x.astype(BF16) for x in u]
    y = [y[h] + _mm(m_rb[h], ub[h], _NN) for h in heads]
    y_ref[...] = jnp.concatenate(y, axis=1)
    for h in heads:
        s_new = s0[h] + _mm(ub[h], bm[h], _TN) + _mm(vm[h], km[h], _TN)
        s_scr[h] = s_new * p_end[:, h * DH_RWKV:(h + 1) * DH_RWKV]

    @pl.when(c == pl.num_programs(0) - 1)
    def _():
        s_out[...] = s_scr[...]


def _rwkv_scan(r, lw, k, v, a, b, *, t):
    spec = pl.BlockSpec((SCAN_CHUNK, C_RWKV), lambda c: (c, 0))
    st = (H_RWKV, DH_RWKV, DH_RWKV)
    return pl.pallas_call(
        _rwkv_scan_kernel,
        grid=(t // SCAN_CHUNK,),
        in_specs=[spec] * 6,
        out_specs=[spec, pl.BlockSpec(st, lambda c: (0, 0, 0))],
        out_shape=[jax.ShapeDtypeStruct((t, C_RWKV), F32), jax.ShapeDtypeStruct(st, F32)],
        scratch_shapes=[pltpu.VMEM(st, F32)],
        compiler_params=_cparams(("arbitrary",)),
        name="rwkv_scan",
    )(r, lw, k, v, a, b)


def _rwkv_step_kernel(r_ref, lw_ref, k_ref, v_ref, a_ref, b_ref, s_ref, y_ref, s_out):
    s = s_ref[...]
    dv, dk = s.shape[-2:]
    eye = (lax.broadcasted_iota(jnp.int32, (dv, dk), 0) == lax.broadcasted_iota(jnp.int32, (dv, dk), 1))
    sa = jnp.sum(s * a_ref[...], axis=-1, keepdims=True)
    v_col = jnp.sum(jnp.where(eye, v_ref[...], 0.0), axis=-1, keepdims=True)
    s_new = s * jnp.exp(lw_ref[...]) + sa * b_ref[...] + v_col * k_ref[...]
    y_col = jnp.sum(s_new * r_ref[...], axis=-1, keepdims=True)
    y_ref[...] = jnp.sum(jnp.where(eye, y_col, 0.0), axis=-2, keepdims=True)
    s_out[...] = s_new


def _rwkv_step(r, lw, k, v, a, b, state, *, nb):
    n = state.shape[0]
    vec = lambda z: z.reshape(n, H_RWKV, 1, DH_RWKV)
    vspec = pl.BlockSpec((nb, H_RWKV, 1, DH_RWKV), lambda i: (i, 0, 0, 0))
    sspec = pl.BlockSpec((nb, H_RWKV, DH_RWKV, DH_RWKV), lambda i: (i, 0, 0, 0))
    y, s_new = pl.pallas_call(
        _rwkv_step_kernel,
        grid=(n // nb,),
        in_specs=[vspec] * 6 + [sspec],
        out_specs=[vspec, sspec],
        out_shape=[jax.ShapeDtypeStruct((n, H_RWKV, 1, DH_RWKV), F32),
                   jax.ShapeDtypeStruct(state.shape, F32)],
        compiler_params=_cparams(("parallel",)),
        name="rwkv_step",
    )(vec(r), vec(lw), vec(k), vec(v), vec(a), vec(b), state)
    return y.reshape(n, C_RWKV), s_new


def _rwkv_post_kernel(yp_ref, ys_ref, g_ref, bonus_ref, lng_ref, lnb_ref, e_ref, et_ref, o_ref, *, seq_tiles):
    y = jnp.where(pl.program_id(0) < seq_tiles, yp_ref[...], ys_ref[...])
    inv_n = 1.0 / DH_RWKV
    mean = _seg_sum(y, e_ref, et_ref) * inv_n
    yc = y - mean
    var = _seg_sum(yc * yc, e_ref, et_ref) * inv_n
    yn = yc * lax.rsqrt(var + LN_X_EPS) * lng_ref[...] + lnb_ref[...]
    o_ref[...] = ((yn + bonus_ref[...]) * g_ref[...]).astype(o_ref.dtype)


def _rwkv_post(y_seq, y_rows, g, bonus, ln_g, ln_b, sel, *, tm):
    m = g.shape[0]
    s_len = y_seq.shape[0]
    assert s_len % tm == 0 and y_rows.shape[0] % tm == 0
    seq_tiles = s_len // tm
    e, et = sel
    row = pl.BlockSpec((tm, C_RWKV), lambda i: (i, 0))
    vec = pl.BlockSpec((1, C_RWKV), lambda i: (0, 0))
    return pl.pallas_call(
        functools.partial(_rwkv_post_kernel, seq_tiles=seq_tiles),
        grid=(m // tm,),
        in_specs=[pl.BlockSpec((tm, C_RWKV), lambda i: (jnp.minimum(i, seq_tiles - 1), 0)),
                  pl.BlockSpec((tm, C_RWKV), lambda i: (jnp.maximum(i - seq_tiles, 0), 0)),
                  row, row, vec, vec,
                  pl.BlockSpec(e.shape, lambda i: (0, 0)), pl.BlockSpec(et.shape, lambda i: (0, 0))],
        out_specs=row,
        out_shape=jax.ShapeDtypeStruct((m, C_RWKV), BF16),
        compiler_params=_cparams(("parallel",)),
        name="rwkv_post",
    )(y_seq, y_rows, g, bonus, ln_g, ln_b, e, et)


def _pad_cols(w, width):
    return jnp.pad(w, ((0, 0), (0, width - w.shape[1])))


def _pad_rows(w, height):
    return jnp.pad(w, ((0, height - w.shape[0]), (0, 0)))


def _repack_rwkv_cols(w):
    wl0 = 3 * C_RWKV
    al0 = wl0 + W_LORA
    gl0 = al0 + A_LORA
    return jnp.concatenate([
        w[:, :wl0],
        _pad_cols(w[:, wl0:al0], LANES),
        _pad_cols(w[:, al0:gl0], LANES),
        _pad_cols(w[:, gl0:], 2 * LANES),
    ], axis=1)


def _unpack_rwkv_cols(w):
    return jnp.concatenate([
        w[:, :RP_WL], w[:, RP_WL:RP_WL + W_LORA], w[:, RP_AL:RP_AL + A_LORA], w[:, RP_GL:RP_GL + G_LORA],
    ], axis=1)


def _head_selectors():
    lane_head = jnp.arange(C_RWKV) // DH_RWKV
    e = (lane_head[:, None] == jnp.arange(LANES)[None, :]).astype(BF16)
    return e, e.T


def kernel(x_prompt, x_sample, cache_k, cache_v, cache_logf, state_wkv, state_shift, page_table,
           p_prompt, p_sample,
           ffn1_norm, ffn1_w_gate, ffn1_w_up, ffn1_w_down, mix_norm, w_in, fox_b_f,
           rwkv_mu, rwkv_w0, rwkv_w2, rwkv_a0, rwkv_a2, rwkv_g2, rwkv_k_k, rwkv_k_a, rwkv_r_k,
           rwkv_ln_g, rwkv_ln_b, w_o_fox, w_o_rwkv, w_out,
           ffn2_norm, ffn2_w_gate, ffn2_w_up, ffn2_w_down, ple_norm, ple_w_gate, ple_w_proj,
           final_norm):
    depth = w_in.shape[0]
    n_pr, s_len, d_model = x_prompt.shape
    n_dec = x_sample.shape[0]
    assert n_pr == 1 and x_sample.shape[1] == 1
    m_rows = s_len + n_dec
    n_pages = page_table.shape[1]
    pps = DECODE_PAGES_PER_STEP if n_pages % DECODE_PAGES_PER_STEP == 0 else 1
    tm = 640 if m_rows % 640 == 0 else LANES
    tm_wide = 1664 if m_rows % 1664 == 0 else tm
    tq = min(512, s_len)
    t_rwkv = n_dec

    off_f = 3 * C_FOX
    off_r = off_f + H_FOX
    off_g = off_r + N_RWKV
    qkv_scale = jnp.concatenate([jnp.full((1, C_FOX), LOG2E * DH_FOX ** -0.5, F32), jnp.ones((1, 2 * C_FOX), F32)], axis=1)
    sel = _head_selectors()
    row2 = lambda z: z.reshape(1, -1).astype(F32)

    x = jnp.concatenate([x_prompt[0], x_sample[:, 0]], axis=0)
    outs = {name: [] for name in ("kp", "vp", "lfp", "wkvp", "shp", "ks", "vs", "lfs", "wkvs", "shs")}
    ffn1_w = [_to_bf16(w) for w in (ffn1_w_gate, ffn1_w_up, ffn1_w_down)]
    ffn2_w = [_to_bf16(w) for w in (ffn2_w_gate, ffn2_w_up, ffn2_w_down)]
    mix_w = [_to_bf16(w) for w in (w_o_fox, w_o_rwkv, w_out)]
    ple_w = [_to_bf16(w) for w in (ple_w_gate, ple_w_proj)]
    w_in_bf = _to_bf16(w_in)
    for i in range(depth):
        w_i = w_in_bf[i]
        w_qkv = w_i[:, :off_f]
        w_f = _pad_cols(w_i[:, off_f:off_r], LANES)
        w_r = _repack_rwkv_cols(w_i[:, off_r:off_g])
        w_g = w_i[:, off_g:]
        b_f = _pad_cols(row2(fox_b_f[i]), LANES)

        x, h = _ffn(i, x, row2(ffn1_norm[i]), row2(mix_norm[i]), *ffn1_w, tm=tm, tf=512)

        qkv, qkv_bf = _proj(h, w_qkv, qkv_scale, mode="qkv", tm=tm_wide, tn=512)
        logf = _proj(h, w_f, b_f, mode="logf", tm=tm_wide, tn=LANES)[0][:, :H_FOX]
        proj_r = _proj(h, w_r, jnp.zeros((1, RP_W), F32), mode="plain", tm=tm_wide, tn=512)[0]
        gates = _proj(h, w_g, jnp.zeros((1, 2 * d_model), F32), mode="sigmoid", tm=tm_wide, tn=1024)[0]

        lf_p = logf[:s_len]
        c_row = _cumsum_rows(lf_p.T)
        o_fox_p = _fox_prompt(qkv_bf, c_row.T, s_len=s_len, tq=tq, tk=tq)
        q_s = qkv[s_len:, :C_FOX].reshape(n_dec, H_FOX, DH_FOX)
        k_s = qkv[s_len:, C_FOX:2 * C_FOX].reshape(n_dec, H_FOX, DH_FOX)
        v_s = qkv[s_len:, 2 * C_FOX:].reshape(n_dec, H_FOX, DH_FOX)
        lf_s = logf[s_len:]
        o_fox_s = _fox_decode(
            i, page_table, q_s, k_s, v_s, jnp.tile(lf_s, (1, LANES // H_FOX)).reshape(n_dec, 1, LANES),
            cache_k, cache_v, cache_logf, pps=pps)
        o_fox = jnp.concatenate([o_fox_p, o_fox_s.reshape(n_dec, C_FOX).astype(BF16)], axis=0)

        vecs = (_repack_rwkv_cols(row2(rwkv_mu[i])), row2(rwkv_w0[i]), row2(rwkv_a0[i]),
                row2(rwkv_k_k[i]), row2(rwkv_k_a[i]), row2(rwkv_r_k[i]))
        mats = (_pad_rows(rwkv_w2[i], LANES).astype(BF16), _pad_rows(rwkv_a2[i], LANES).astype(BF16),
                _pad_rows(rwkv_g2[i], 2 * LANES).astype(BF16))
        prev_s = _repack_rwkv_cols(state_shift[i].astype(F32))
        pre = _rwkv_prep(proj_r, prev_s, vecs, mats, sel, s_len=s_len, tm=t_rwkv)
        y_p, wkv_p = _rwkv_scan(*pre[:6], t=s_len)
        y_s, wkv_s = _rwkv_step(*[z[s_len:] for z in pre[:6]], state_wkv[i].astype(F32), nb=8)
        o_rwkv = _rwkv_post(y_p, y_s, pre[6], pre[7], row2(rwkv_ln_g[i]), row2(rwkv_ln_b[i]), sel, tm=t_rwkv)

        x = _merge(i, x, o_fox, o_rwkv, gates, *mix_w, tm=tm, tn=512)
        x, u = _ffn(i, x, row2(ffn2_norm[i]), row2(ple_norm[i]), *ffn2_w, tm=tm, tf=512)
        p_all = jnp.concatenate([p_prompt[i, 0], p_sample[i, :, 0]], axis=0)
        x = _ple(i, x, u, p_all, *ple_w, tm=tm_wide, tn=512)

        outs["kp"].append(qkv[:s_len, C_FOX:2 * C_FOX].reshape(1, s_len, H_FOX, DH_FOX))
        outs["vp"].append(qkv[:s_len, 2 * C_FOX:].reshape(1, s_len, H_FOX, DH_FOX))
        outs["lfp"].append(lf_p.reshape(1, s_len, H_FOX))
        outs["wkvp"].append(wkv_p[None])
        outs["shp"].append(_unpack_rwkv_cols(proj_r[s_len - 1:s_len]))
        outs["ks"].append(k_s.reshape(n_dec, 1, H_FOX, DH_FOX))
        outs["vs"].append(v_s.reshape(n_dec, 1, H_FOX, DH_FOX))
        outs["lfs"].append(lf_s.reshape(n_dec, 1, H_FOX))
        outs["wkvs"].append(wkv_s)
        outs["shs"].append(_unpack_rwkv_cols(proj_r[s_len:]))

    y = _final_norm(x, row2(final_norm), tm=tm)
    st = {k: jnp.stack(v) for k, v in outs.items()}
    return (y[:s_len].reshape(1, s_len, d_model), y[s_len:].reshape(n_dec, 1, d_model),
            st["kp"], st["vp"], st["lfp"], st["wkvp"], st["shp"],
            st["ks"], st["vs"], st["lfs"], st["wkvs"], st["shs"])
```

```python
import functools

import jax
import jax.numpy as jnp
from jax import lax
from jax.experimental import pallas as pl
from jax.experimental.pallas import tpu as pltpu

F32 = jnp.float32
BF16 = jnp.bfloat16

NORM_EPS = 1e-6
LN_X_EPS = 64e-5
KK_EPS = 1e-12

LANES = 128
VMEM_LIMIT = 56 * 1024 * 1024

H_FOX = 8
DH_FOX = 128
C_FOX = H_FOX * DH_FOX
H_RWKV = 16
DH_RWKV = 64
C_RWKV = H_RWKV * DH_RWKV
W_LORA = 64
A_LORA = 64
G_LORA = 160
N_RWKV = 3 * C_RWKV + W_LORA + A_LORA + G_LORA
RP_WL = 3 * C_RWKV
RP_AL = RP_WL + LANES
RP_GL = RP_AL + LANES
RP_W = RP_GL + 2 * LANES
LOG2E = 1.4426950408889634
DECODE_PAGES_PER_STEP = 16
SCAN_CHUNK = 64


def _cparams(sem):
    return pltpu.CompilerParams(dimension_semantics=sem, vmem_limit_bytes=VMEM_LIMIT)


def _sigmoid(x):
    return 1.0 / (1.0 + jnp.exp(-x))


def _softplus(x):
    return jnp.maximum(x, 0.0) + jnp.log(1.0 + jnp.exp(-jnp.abs(x)))


def _rms_rows(x, g):
    ms = jnp.mean(x * x, axis=-1, keepdims=True)
    return x * lax.rsqrt(ms + NORM_EPS) * g


def _dot(a, b):
    return jnp.dot(a, b, preferred_element_type=F32)


def _split2(x):
    hi = x.astype(BF16)
    lo = (x - hi.astype(F32)).astype(BF16)
    return hi, lo


def _ffn_kernel(x_ref, g_ref, gn_ref, wg_ref, wu_ref, wd_ref, o_ref, hn_ref, h_scr, acc_scr):
    f = pl.program_id(1)

    @pl.when(f == 0)
    def _():
        h_scr[...] = _rms_rows(x_ref[...], g_ref[...]).astype(BF16)
        acc_scr[...] = jnp.zeros_like(acc_scr)

    h = h_scr[...]
    gate = _dot(h, wg_ref[...])
    up = _dot(h, wu_ref[...])
    act = (gate * _sigmoid(gate) * up).astype(BF16)
    acc_scr[...] += _dot(act, wd_ref[...])

    @pl.when(f == pl.num_programs(1) - 1)
    def _():
        out = x_ref[...] + 0.5 * acc_scr[...]
        o_ref[...] = out
        hn_ref[...] = _rms_rows(out, gn_ref[...]).astype(BF16)


def _cast_kernel(w_ref, o_ref):
    o_ref[...] = w_ref[...].astype(o_ref.dtype)


def _to_bf16(w):
    depth, rows, cols = w.shape
    tr = next(t for t in (512, 256, 128, 64, 32, 16) if rows % t == 0 and t * cols * 4 <= 6 * 2 ** 20)
    spec = pl.BlockSpec((None, tr, cols), lambda l, r: (l, r, 0))
    return pl.pallas_call(
        _cast_kernel,
        grid=(depth, rows // tr),
        in_specs=[spec],
        out_specs=spec,
        out_shape=jax.ShapeDtypeStruct(w.shape, BF16),
        compiler_params=_cparams(("parallel", "parallel")),
        name="to_bf16",
    )(w)


def _ffn(layer, x, g, g_next, wg, wu, wd, *, tm, tf):
    m, d = x.shape
    dff = wg.shape[2]
    row = pl.BlockSpec((tm, d), lambda i, f: (i, 0))
    vec = pl.BlockSpec((1, d), lambda i, f: (0, 0))
    return pl.pallas_call(
        _ffn_kernel,
        grid=(m // tm, dff // tf),
        in_specs=[
            row, vec, vec,
            pl.BlockSpec((None, d, tf), lambda i, f: (layer, 0, f)),
            pl.BlockSpec((None, d, tf), lambda i, f: (layer, 0, f)),
            pl.BlockSpec((None, tf, d), lambda i, f: (layer, f, 0)),
        ],
        out_specs=[row, row],
        out_shape=[jax.ShapeDtypeStruct((m, d), F32), jax.ShapeDtypeStruct((m, d), BF16)],
        scratch_shapes=[pltpu.VMEM((tm, d), BF16), pltpu.VMEM((tm, d), F32)],
        compiler_params=_cparams(("parallel", "arbitrary")),
        name="ffn",
    )(x, g, g_next, wg, wu, wd)


def _proj_kernel(h_ref, w_ref, aux_ref, *outs, mode):
    acc = _dot(h_ref[...], w_ref[...])
    if mode == "qkv":
        outs[0][...] = acc
        outs[1][...] = (acc * aux_ref[...]).astype(BF16)
    elif mode == "logf":
        outs[0][...] = -_softplus(-(acc + aux_ref[...]))
    elif mode == "sigmoid":
        outs[0][...] = _sigmoid(acc)
    else:
        outs[0][...] = acc


def _proj(h, w, aux, *, mode, tm, tn):
    m, d = h.shape
    n = w.shape[1]
    out_shape = [jax.ShapeDtypeStruct((m, n), F32)]
    if mode == "qkv":
        out_shape.append(jax.ShapeDtypeStruct((m, n), BF16))
    o_spec = pl.BlockSpec((tm, tn), lambda i, j: (i, j))
    return pl.pallas_call(
        functools.partial(_proj_kernel, mode=mode),
        grid=(m // tm, n // tn),
        in_specs=[
            pl.BlockSpec((tm, d), lambda i, j: (i, 0)),
            pl.BlockSpec((d, tn), lambda i, j: (0, j)),
            pl.BlockSpec((1, tn), lambda i, j: (0, j)),
        ],
        out_specs=[o_spec] * len(out_shape),
        out_shape=out_shape,
        compiler_params=_cparams(("parallel", "arbitrary")),
        name="proj_" + mode,
    )(h, w, aux)


def _merge_kernel(x_ref, of_ref, or_ref, gf_ref, gr_ref, wof_ref, wor_ref, wout_ref, o_ref, acc_scr):
    n = pl.program_id(1)

    @pl.when(n == 0)
    def _():
        acc_scr[...] = jnp.zeros_like(acc_scr)

    merged = gf_ref[...] * _dot(of_ref[...], wof_ref[...]) + gr_ref[...] * _dot(or_ref[...], wor_ref[...])
    acc_scr[...] += _dot(merged.astype(BF16), wout_ref[...])

    @pl.when(n == pl.num_programs(1) - 1)
    def _():
        o_ref[...] = x_ref[...] + acc_scr[...]


def _merge(layer, x, o_fox, o_rwkv, gates, wof, wor, wout, *, tm, tn):
    m, d = x.shape
    nb = d // tn
    return pl.pallas_call(
        _merge_kernel,
        grid=(m // tm, nb),
        in_specs=[
            pl.BlockSpec((tm, d), lambda i, n: (i, 0)),
            pl.BlockSpec((tm, o_fox.shape[1]), lambda i, n: (i, 0)),
            pl.BlockSpec((tm, o_rwkv.shape[1]), lambda i, n: (i, 0)),
            pl.BlockSpec((tm, tn), lambda i, n: (i, n)),
            pl.BlockSpec((tm, tn), lambda i, n: (i, n + nb)),
            pl.BlockSpec((None, wof.shape[1], tn), lambda i, n: (layer, 0, n)),
            pl.BlockSpec((None, wor.shape[1], tn), lambda i, n: (layer, 0, n)),
            pl.BlockSpec((None, tn, d), lambda i, n: (layer, n, 0)),
        ],
        out_specs=pl.BlockSpec((tm, d), lambda i, n: (i, 0)),
        out_shape=jax.ShapeDtypeStruct((m, d), F32),
        scratch_shapes=[pltpu.VMEM((tm, d), F32)],
        compiler_params=_cparams(("parallel", "arbitrary")),
        name="merge",
    )(x, o_fox, o_rwkv, gates, gates, wof, wor, wout)


def _ple_kernel(x_ref, u_ref, p_ref, wg_ref, wp_ref, o_ref):
    gate = _sigmoid(_dot(u_ref[...], wg_ref[...]))
    emb = _dot(p_ref[...].astype(BF16), wp_ref[...])
    o_ref[...] = x_ref[...] + gate * emb


def _ple(layer, x, u, p, wg, wp, *, tm, tn):
    m, d = x.shape
    return pl.pallas_call(
        _ple_kernel,
        grid=(m // tm, d // tn),
        in_specs=[
            pl.BlockSpec((tm, tn), lambda i, j: (i, j)),
            pl.BlockSpec((tm, d), lambda i, j: (i, 0)),
            pl.BlockSpec((tm, p.shape[1]), lambda i, j: (i, 0)),
            pl.BlockSpec((None, d, tn), lambda i, j: (layer, 0, j)),
            pl.BlockSpec((None, p.shape[1], tn), lambda i, j: (layer, 0, j)),
        ],
        out_specs=pl.BlockSpec((tm, tn), lambda i, j: (i, j)),
        out_shape=jax.ShapeDtypeStruct((m, d), F32),
        compiler_params=_cparams(("parallel", "arbitrary")),
        name="ple",
    )(x, u, p, wg, wp)


def _norm_kernel(x_ref, g_ref, o_ref):
    o_ref[...] = _rms_rows(x_ref[...], g_ref[...])


def _final_norm(x, g, *, tm):
    m, d = x.shape
    return pl.pallas_call(
        _norm_kernel,
        grid=(m // tm,),
        in_specs=[pl.BlockSpec((tm, d), lambda i: (i, 0)), pl.BlockSpec((1, d), lambda i: (0, 0))],
        out_specs=pl.BlockSpec((tm, d), lambda i: (i, 0)),
        out_shape=jax.ShapeDtypeStruct((m, d), F32),
        compiler_params=_cparams(("parallel",)),
        name="final_norm",
    )(x, g)


def _cumsum_kernel(lf_ref, o_ref):
    h, s = lf_ref.shape
    row = lax.broadcasted_iota(jnp.int32, (LANES, LANES), 0)
    col = lax.broadcasted_iota(jnp.int32, (LANES, LANES), 1)
    tri = jnp.where(row <= col, 1.0, 0.0)

    def body(c, carry):
        off = pl.multiple_of(c * LANES, LANES)
        y = jnp.dot(lf_ref[:, pl.ds(off, LANES)], tri, precision=lax.Precision.HIGHEST,
                    preferred_element_type=F32) + carry
        o_ref[:, pl.ds(off, LANES)] = y
        return y[:, LANES - 1:LANES]

    lax.fori_loop(0, s // LANES, body, jnp.zeros((h, 1), F32))


def _cumsum_rows(lf_t):
    return pl.pallas_call(
        _cumsum_kernel,
        out_shape=jax.ShapeDtypeStruct(lf_t.shape, F32),
        compiler_params=pltpu.CompilerParams(vmem_limit_bytes=VMEM_LIMIT),
        name="logf_cumsum",
    )(lf_t)


def _fox_prompt_kernel(q_ref, k_ref, v_ref, qx_ref, kx_ref, o_ref, m_scr, l_scr, acc_scr, *, tq, tk):
    qi = pl.program_id(0)
    kj = pl.program_id(1)

    @pl.when(kj == 0)
    def _():
        m_scr[...] = jnp.full_like(m_scr, -jnp.inf)
        l_scr[...] = jnp.zeros_like(l_scr)
        acc_scr[...] = jnp.zeros_like(acc_scr)

    def update(on_diagonal):
        if on_diagonal:
            causal = (lax.broadcasted_iota(jnp.int32, (tq, tk), 1)
                      <= lax.broadcasted_iota(jnp.int32, (tq, tk), 0))
        heads = range(H_FOX)
        sl = [slice(h * DH_FOX, (h + 1) * DH_FOX) for h in heads]
        ones = jnp.ones((tk, LANES), BF16)
        t = []
        for h in heads:
            q_wide = jnp.concatenate([q_ref[:, sl[h]], qx_ref[h]], axis=1)
            k_wide = jnp.concatenate([k_ref[:, sl[h]], kx_ref[h]], axis=1)
            th = lax.dot_general(q_wide, k_wide, (_NT, ((), ())), preferred_element_type=F32)
            t.append(jnp.where(causal, th, -jnp.inf) if on_diagonal else th)
        m_old = [m_scr[h] for h in heads]
        m_new = [jnp.maximum(m_old[h], jnp.max(t[h], axis=-1, keepdims=True)) for h in heads]
        alpha = [jnp.exp2(m_old[h] - m_new[h]) for h in heads]
        p = [jnp.exp2(t[h] - jnp.tile(m_new[h], (1, tk // LANES))).astype(BF16) for h in heads]
        for h in heads:
            pv = _dot(p[h], jnp.concatenate([v_ref[:, sl[h]], ones], axis=1))
            acc_scr[h] = alpha[h] * acc_scr[h] + pv[:, :DH_FOX]
            l_scr[h] = alpha[h] * l_scr[h] + pv[:, DH_FOX:]
            m_scr[h] = m_new[h]

    @pl.when(kj < qi)
    def _():
        update(False)

    @pl.when(kj == qi)
    def _():
        update(True)
        for h in range(H_FOX):
            o_ref[:, h * DH_FOX:(h + 1) * DH_FOX] = (acc_scr[h] / l_scr[h]).astype(o_ref.dtype)


def _forget_blocks(c_col):
    def top16(x):
        bits = lax.bitcast_convert_type(x, jnp.uint32) & jnp.uint32(0xFFFF0000)
        return lax.bitcast_convert_type(bits, F32)

    c2 = (c_col * LOG2E).T[:, :, None]
    hi = top16(c2)
    mid = top16(c2 - hi)
    lo = c2 - hi - mid
    lane = lax.broadcasted_iota(jnp.int32, c2.shape[:2] + (LANES,), 2)
    part = lambda j: jnp.where(lane == j, hi, jnp.where(lane == j + 1, mid, jnp.where(lane == j + 2, lo, 0.0)))
    one = lambda j: jnp.where((lane >= j) & (lane < j + 3), 1.0, 0.0)
    qx = (part(0) + one(3)).astype(BF16)
    kx = (one(0) - part(3)).astype(BF16)
    return qx, kx


def _fox_prompt(qkv_bf, c_col, *, s_len, tq, tk):
    assert tq == tk and DH_FOX == LANES
    nq, nk = s_len // tq, s_len // tk
    qx, kx = _forget_blocks(c_col)
    return pl.pallas_call(
        functools.partial(_fox_prompt_kernel, tq=tq, tk=tk),
        grid=(nq, nk),
        in_specs=[
            pl.BlockSpec((tq, C_FOX), lambda i, j: (i, 0)),
            pl.BlockSpec((tk, C_FOX), lambda i, j: (jnp.minimum(i, j), 1)),
            pl.BlockSpec((tk, C_FOX), lambda i, j: (jnp.minimum(i, j), 2)),
            pl.BlockSpec((H_FOX, tq, LANES), lambda i, j: (0, i, 0)),
            pl.BlockSpec((H_FOX, tk, LANES), lambda i, j: (0, jnp.minimum(i, j), 0)),
        ],
        out_specs=pl.BlockSpec((tq, C_FOX), lambda i, j: (i, 0)),
        out_shape=jax.ShapeDtypeStruct((s_len, C_FOX), BF16),
        scratch_shapes=[
            pltpu.VMEM((H_FOX, tq, LANES), F32),
            pltpu.VMEM((H_FOX, tq, LANES), F32),
            pltpu.VMEM((H_FOX, tq, DH_FOX), F32),
        ],
        compiler_params=_cparams(("parallel", "arbitrary")),
        name="fox_prompt",
    )(qkv_bf, qkv_bf, qkv_bf, qx, kx)


def _fox_decode_kernel(pt_ref, q_ref, kn_ref, vn_ref, lfn_ref, sfx_ref, *rest, scale, pps):
    del pt_ref
    kc, vc, lfc = rest[:pps], rest[pps:2 * pps], rest[2 * pps:3 * pps]
    o_ref, m_scr, l_scr, acc_scr, carry_scr = rest[3 * pps:]
    p = pl.program_id(1)
    page = kc[0].shape[0]
    nrow = page * H_FOX
    sub = lax.broadcasted_iota(jnp.int32, (H_FOX, nrow + LANES), 0)
    lane = lax.broadcasted_iota(jnp.int32, (H_FOX, nrow + LANES), 1)
    own_ext = (lane % H_FOX) == sub
    own = own_ext[:, :nrow]
    pick = lane[:, :LANES] == sub[:, :LANES]
    q8 = q_ref[...] * scale

    def to_col(row):
        return jnp.sum(jnp.where(pick, row, 0.0), axis=-1, keepdims=True)

    def per_head(row, op):
        parts = [row[:, j * LANES:(j + 1) * LANES] for j in range(nrow // LANES)]
        while len(parts) > 1:
            parts = [op(parts[i], parts[i + 1]) for i in range(0, len(parts), 2)]
        acc = parts[0]
        shift = H_FOX
        while shift < LANES:
            acc = op(acc, pltpu.roll(acc, shift, axis=1))
            shift *= 2
        return acc

    @pl.when(p == 0)
    def _():
        m_scr[...] = jnp.full_like(m_scr, -jnp.inf)
        l_scr[...] = jnp.zeros_like(l_scr)
        acc_scr[...] = jnp.zeros_like(acc_scr)
        carry_scr[...] = lfn_ref[...]

    q8_bf = q8.astype(BF16)
    widen = lambda row: jnp.tile(row, (1, nrow // LANES))
    carry = carry_scr[...]
    scores = []
    for j in range(pps):
        k2 = kc[j][...].reshape(nrow, DH_FOX).astype(BF16)
        g = lax.dot_general(q8_bf, k2, (_NT, ((), ())), preferred_element_type=F32)
        s = jnp.sum(jnp.where(own, g, 0.0), axis=0, keepdims=True)

        lf = lfc[j][...]
        hi = lf.astype(BF16).astype(F32)
        lf2 = jnp.concatenate([hi, lf - hi], axis=1).astype(BF16)
        t = _dot(lf2, sfx_ref[...])
        bias = jnp.sum(jnp.where(own_ext, t, 0.0), axis=0, keepdims=True)
        scores.append(s + bias[:, :nrow] + widen(carry))
        carry = carry + bias[:, nrow:]
    carry_scr[...] = carry

    m_old = m_scr[...]
    m_new = m_old
    for s in scores:
        m_new = jnp.maximum(m_new, per_head(s, jnp.maximum))
    alpha = jnp.exp(m_old - m_new)
    m_wide = widen(m_new)
    l_new = alpha * l_scr[...]
    acc = to_col(alpha) * acc_scr[...]
    for j, s in enumerate(scores):
        pr = jnp.exp(s - m_wide)
        l_new = l_new + per_head(pr, jnp.add)
        p8 = jnp.where(own, pr, 0.0).astype(BF16)
        acc = acc + _dot(p8, vc[j][...].reshape(nrow, DH_FOX).astype(BF16))
    l_scr[...] = l_new
    acc_scr[...] = acc
    m_scr[...] = m_new

    @pl.when(p == pl.num_programs(1) - 1)
    def _():
        s_new = jnp.sum(q8 * kn_ref[...], axis=-1, keepdims=True)
        m_old = to_col(m_scr[...])
        m_fin = jnp.maximum(m_old, s_new)
        alpha = jnp.exp(m_old - m_fin)
        p_new = jnp.exp(s_new - m_fin)
        l_fin = alpha * to_col(l_scr[...]) + p_new
        o_ref[...] = (alpha * acc_scr[...] + p_new * vn_ref[...]) / l_fin


def _suffix_selector(page):
    r = jnp.arange(page * H_FOX + LANES)
    tok = jnp.arange(page)
    later = jnp.where(r[None, :] < page * H_FOX, tok[:, None] > r[None, :] // H_FOX, True)
    return jnp.tile(later.astype(BF16), (2, 1))


def _fox_decode(layer, page_table, q, k_new, v_new, lf_new, cache_k, cache_v, cache_lf, *, pps):
    n, n_pages = page_table.shape
    page = cache_k.shape[2]
    pt_flat = page_table.reshape(-1)
    sfx = _suffix_selector(page)

    def page_map(j, tail):
        def index_map(b, p, pt):
            return (layer, pt[b * n_pages + (n_pages - 1 - (p * pps + j))]) + tail
        return index_map

    row_spec = pl.BlockSpec((None, H_FOX, DH_FOX), lambda b, p, pt: (b, 0, 0))
    kv_specs = [pl.BlockSpec((None, None, page, H_FOX, DH_FOX), page_map(j, (0, 0, 0))) for j in range(pps)]
    lf_specs = [pl.BlockSpec((None, None, H_FOX, page), page_map(j, (0, 0))) for j in range(pps)]
    grid_spec = pltpu.PrefetchScalarGridSpec(
        num_scalar_prefetch=1,
        grid=(n, n_pages // pps),
        in_specs=[
            row_spec, row_spec, row_spec,
            pl.BlockSpec((None, 1, LANES), lambda b, p, pt: (b, 0, 0)),
            pl.BlockSpec(sfx.shape, lambda b, p, pt: (0, 0)),
        ] + kv_specs + kv_specs + lf_specs,
        out_specs=row_spec,
        scratch_shapes=[
            pltpu.VMEM((1, LANES), F32),
            pltpu.VMEM((1, LANES), F32),
            pltpu.VMEM((H_FOX, DH_FOX), F32),
            pltpu.VMEM((1, LANES), F32),
        ],
    )
    return pl.pallas_call(
        functools.partial(_fox_decode_kernel, scale=DH_FOX ** -0.5, pps=pps),
        grid_spec=grid_spec,
        out_shape=jax.ShapeDtypeStruct((n, H_FOX, DH_FOX), F32),
        compiler_params=_cparams(("parallel", "arbitrary")),
        name="fox_decode",
    )(pt_flat, q, k_new, v_new, lf_new, sfx, *([cache_k] * pps), *([cache_v] * pps), *([cache_lf] * pps))


def _seg_sum(x, e_ref, et_ref):
    hi, lo = _split2(x)
    s = _dot(hi, e_ref[...]) + _dot(lo, e_ref[...])
    shi, slo = _split2(s)
    return _dot(shi, et_ref[...]) + _dot(slo, et_ref[...])


def _rwkv_prep_kernel(pr_ref, pv_ref, ps_ref, mu_ref, w0_ref, a0_ref, kk_ref, ka_ref, rk_ref,
                      w2_ref, a2_ref, g2_ref, e_ref, et_ref,
                      r_out, lw_out, k_out, v_out, a_out, b_out, g_out, bonus_out, *, seq_tiles):
    i = pl.program_id(0)
    pr = pr_ref[...]
    first = jnp.where(i == 0, 0.0, pv_ref[7:8, :])
    prev = jnp.where(lax.broadcasted_iota(jnp.int32, pr.shape, 0) == 0, first, pltpu.roll(pr, 1, axis=0))
    prev = jnp.where(i < seq_tiles, prev, ps_ref[...])
    xm = pr + (prev - pr) * mu_ref[...]
    r = xm[:, 0:C_RWKV]
    k = xm[:, C_RWKV:2 * C_RWKV]
    v = xm[:, 2 * C_RWKV:3 * C_RWKV]
    wl = xm[:, RP_WL:RP_AL]
    al = xm[:, RP_AL:RP_GL]
    gl = xm[:, RP_GL:RP_W]

    w_log = -_softplus(-(w0_ref[...] + _dot(jnp.tanh(wl).astype(BF16), w2_ref[...]))) - 0.5
    a_sig = _sigmoid(a0_ref[...] + _dot(al.astype(BF16), a2_ref[...]))
    g = _dot(_sigmoid(gl).astype(BF16), g2_ref[...])

    kk = k * kk_ref[...]
    norm = jnp.sqrt(_seg_sum(kk * kk, e_ref, et_ref))
    kk = kk / jnp.maximum(norm, KK_EPS)
    k_mod = k * (1.0 + (a_sig - 1.0) * ka_ref[...])

    r_out[...] = r
    lw_out[...] = -jnp.exp(w_log)
    k_out[...] = k_mod
    v_out[...] = v
    a_out[...] = -kk
    b_out[...] = kk * a_sig
    g_out[...] = g
    bonus_out[...] = _seg_sum(r * k_mod * rk_ref[...], e_ref, et_ref) * v


def _rwkv_prep(proj, prev_rows, vecs, mats, sel, *, s_len, tm):
    m = proj.shape[0]
    assert s_len % tm == 0 and (m - s_len) % tm == 0
    seq_tiles = s_len // tm
    mu, w0, a0, k_k, k_a, r_k = vecs
    w2, a2, g2 = mats
    e, et = sel
    row = lambda i: (i, 0)
    fix = lambda i: (0, 0)
    vec_spec = pl.BlockSpec((1, C_RWKV), fix)
    out = jax.ShapeDtypeStruct((m, C_RWKV), F32)
    return pl.pallas_call(
        functools.partial(_rwkv_prep_kernel, seq_tiles=seq_tiles),
        grid=(m // tm,),
        in_specs=[
            pl.BlockSpec((tm, RP_W), row),
            pl.BlockSpec((8, RP_W), lambda i: (jnp.maximum(i * (tm // 8) - 1, 0), 0)),
            pl.BlockSpec((tm, RP_W), lambda i: (jnp.maximum(i - seq_tiles, 0), 0)),
            pl.BlockSpec((1, RP_W), fix),
            vec_spec, vec_spec, vec_spec, vec_spec, vec_spec,
            pl.BlockSpec(w2.shape, fix), pl.BlockSpec(a2.shape, fix), pl.BlockSpec(g2.shape, fix),
            pl.BlockSpec(e.shape, fix), pl.BlockSpec(et.shape, fix),
        ],
        out_specs=[pl.BlockSpec((tm, C_RWKV), row)] * 8,
        out_shape=[out] * 8,
        compiler_params=_cparams(("parallel",)),
        name="rwkv_prep",
    )(proj, proj, prev_rows, mu, w0, a0, k_k, k_a, r_k, w2, a2, g2, e, et)


_NN = ((1,), (0,))
_NT = ((1,), (1,))
_TN = ((0,), (0,))


def _mm(a, b, dims):
    return lax.dot_general(a, b, (dims, ((), ())), preferred_element_type=F32)


def _rwkv_scan_kernel(r_ref, lw_ref, k_ref, v_ref, a_ref, b_ref, y_ref, s_out, s_scr):
    c = pl.program_id(0)
    ch = r_ref.shape[0]
    heads = range(H_RWKV)

    @pl.when(c == 0)
    def _():
        s_scr[...] = jnp.zeros_like(s_scr)

    row = lax.broadcasted_iota(jnp.int32, (ch, ch), 0)
    col = lax.broadcasted_iota(jnp.int32, (ch, ch), 1)
    incl = row >= col
    strict = row > col

    def cut(x):
        return [x[:, h * DH_RWKV:(h + 1) * DH_RWKV].astype(BF16) for h in heads]

    lw = lw_ref[...]
    cs = jnp.dot(jnp.where(incl, 1.0, 0.0), lw, precision=lax.Precision.HIGHEST, preferred_element_type=F32)
    p_incl = jnp.exp(cs)
    p_inv = jnp.exp(-cs)
    p_excl = jnp.exp(cs - lw)
    rm = cut(r_ref[...] * p_incl)
    am = cut(a_ref[...] * p_excl)
    bm = cut(b_ref[...] * p_inv)
    km = cut(k_ref[...] * p_inv)
    vm = cut(v_ref[...])
    p_end = p_incl[ch - 1:ch, :]
    s0 = [s_scr[h] for h in heads]
    s0b = [s.astype(BF16) for s in s0]

    def masked(xs, ys, keep):
        return [jnp.where(keep, _mm(x, y, _NT), 0.0).astype(BF16) for x, y in zip(xs, ys)]

    l_ab = masked(am, bm, strict)
    l_ak = masked(am, km, strict)
    m_rb = masked(rm, bm, incl)
    m_rk = masked(rm, km, incl)
    u = [_mm(am[h], s0b[h], _NT) + _mm(l_ak[h], vm[h], _NN) for h in heads]
    y = [_mm(rm[h], s0b[h], _NT) + _mm(m_rk[h], vm[h], _NN) for h in heads]
    pw = l_ab
    u = [u[h] + _mm(pw[h], u[h].astype(BF16), _NN) for h in heads]
    span = 2
    while span < ch:
        pw = [_mm(pw[h], pw[h], _NN).astype(BF16) for h in heads]
        u = [u[h] + _mm(pw[h], u[h].astype(BF16), _NN) for h in heads]
        span *= 2
    ub = [x.astype(BF16) for x in u]
    y = [y[h] + _mm(m_rb[h], ub[h], _NN) for h in heads]
    y_ref[...] = jnp.concatenate(y, axis=1)
    for h in heads:
        s_new = s0[h] + _mm(ub[h], bm[h], _TN) + _mm(vm[h], km[h], _TN)
        s_scr[h] = s_new * p_end[:, h * DH_RWKV:(h + 1) * DH_RWKV]

    @pl.when(c == pl.num_programs(0) - 1)
    def _():
        s_out[...] = s_scr[...]


def _rwkv_scan(r, lw, k, v, a, b, *, t):
    spec = pl.BlockSpec((SCAN_CHUNK, C_RWKV), lambda c: (c, 0))
    st = (H_RWKV, DH_RWKV, DH_RWKV)
    return pl.pallas_call(
        _rwkv_scan_kernel,
        grid=(t // SCAN_CHUNK,),
        in_specs=[spec] * 6,
        out_specs=[spec, pl.BlockSpec(st, lambda c: (0, 0, 0))],
        out_shape=[jax.ShapeDtypeStruct((t, C_RWKV), F32), jax.ShapeDtypeStruct(st, F32)],
        scratch_shapes=[pltpu.VMEM(st, F32)],
        compiler_params=_cparams(("arbitrary",)),
        name="rwkv_scan",
    )(r, lw, k, v, a, b)


def _rwkv_step_kernel(r_ref, lw_ref, k_ref, v_ref, a_ref, b_ref, s_ref, y_ref, s_out):
    s = s_ref[...]
    dv, dk = s.shape[-2:]
    eye = (lax.broadcasted_iota(jnp.int32, (dv, dk), 0) == lax.broadcasted_iota(jnp.int32, (dv, dk), 1))
    sa = jnp.sum(s * a_ref[...], axis=-1, keepdims=True)
    v_col = jnp.sum(jnp.where(eye, v_ref[...], 0.0), axis=-1, keepdims=True)
    s_new = s * jnp.exp(lw_ref[...]) + sa * b_ref[...] + v_col * k_ref[...]
    y_col = jnp.sum(s_new * r_ref[...], axis=-1, keepdims=True)
    y_ref[...] = jnp.sum(jnp.where(eye, y_col, 0.0), axis=-2, keepdims=True)
    s_out[...] = s_new


def _rwkv_step(r, lw, k, v, a, b, state, *, nb):
    n = state.shape[0]
    vec = lambda z: z.reshape(n, H_RWKV, 1, DH_RWKV)
    vspec = pl.BlockSpec((nb, H_RWKV, 1, DH_RWKV), lambda i: (i, 0, 0, 0))
    sspec = pl.BlockSpec((nb, H_RWKV, DH_RWKV, DH_RWKV), lambda i: (i, 0, 0, 0))
    y, s_new = pl.pallas_call(
        _rwkv_step_kernel,
        grid=(n // nb,),
        in_specs=[vspec] * 6 + [sspec],
        out_specs=[vspec, sspec],
        out_shape=[jax.ShapeDtypeStruct((n, H_RWKV, 1, DH_RWKV), F32),
                   jax.ShapeDtypeStruct(state.shape, F32)],
        compiler_params=_cparams(("parallel",)),
        name="rwkv_step",
    )(vec(r), vec(lw), vec(k), vec(v), vec(a), vec(b), state)
    return y.reshape(n, C_RWKV), s_new


def _rwkv_post_kernel(yp_ref, ys_ref, g_ref, bonus_ref, lng_ref, lnb_ref, e_ref, et_ref, o_ref, *, seq_tiles):
    y = jnp.where(pl.program_id(0) < seq_tiles, yp_ref[...], ys_ref[...])
    inv_n = 1.0 / DH_RWKV
    mean = _seg_sum(y, e_ref, et_ref) * inv_n
    yc = y - mean
    var = _seg_sum(yc * yc, e_ref, et_ref) * inv_n
    yn = yc * lax.rsqrt(var + LN_X_EPS) * lng_ref[...] + lnb_ref[...]
    o_ref[...] = ((yn + bonus_ref[...]) * g_ref[...]).astype(o_ref.dtype)


def _rwkv_post(y_seq, y_rows, g, bonus, ln_g, ln_b, sel, *, tm):
    m = g.shape[0]
    s_len = y_seq.shape[0]
    assert s_len % tm == 0 and y_rows.shape[0] % tm == 0
    seq_tiles = s_len // tm
    e, et = sel
    row = pl.BlockSpec((tm, C_RWKV), lambda i: (i, 0))
    vec = pl.BlockSpec((1, C_RWKV), lambda i: (0, 0))
    return pl.pallas_call(
        functools.partial(_rwkv_post_kernel, seq_tiles=seq_tiles),
        grid=(m // tm,),
        in_specs=[pl.BlockSpec((tm, C_RWKV), lambda i: (jnp.minimum(i, seq_tiles - 1), 0)),
                  pl.BlockSpec((tm, C_RWKV), lambda i: (jnp.maximum(i - seq_tiles, 0), 0)),
                  row, row, vec, vec,
                  pl.BlockSpec(e.shape, lambda i: (0, 0)), pl.BlockSpec(et.shape, lambda i: (0, 0))],
        out_specs=row,
        out_shape=jax.ShapeDtypeStruct((m, C_RWKV), BF16),
        compiler_params=_cparams(("parallel",)),
        name="rwkv_post",
    )(y_seq, y_rows, g, bonus, ln_g, ln_b, e, et)


def _pad_cols(w, width):
    return jnp.pad(w, ((0, 0), (0, width - w.shape[1])))


def _pad_rows(w, height):
    return jnp.pad(w, ((0, height - w.shape[0]), (0, 0)))


def _repack_rwkv_cols(w):
    wl0 = 3 * C_RWKV
    al0 = wl0 + W_LORA
    gl0 = al0 + A_LORA
    return jnp.concatenate([
        w[:, :wl0],
        _pad_cols(w[:, wl0:al0], LANES),
        _pad_cols(w[:, al0:gl0], LANES),
        _pad_cols(w[:, gl0:], 2 * LANES),
    ], axis=1)


def _unpack_rwkv_cols(w):
    return jnp.concatenate([
        w[:, :RP_WL], w[:, RP_WL:RP_WL + W_LORA], w[:, RP_AL:RP_AL + A_LORA], w[:, RP_GL:RP_GL + G_LORA],
    ], axis=1)


def _head_selectors():
    lane_head = jnp.arange(C_RWKV) // DH_RWKV
    e = (lane_head[:, None] == jnp.arange(LANES)[None, :]).astype(BF16)
    return e, e.T


def kernel(x_prompt, x_sample, cache_k, cache_v, cache_logf, state_wkv, state_shift, page_table,
           p_prompt, p_sample,
           ffn1_norm, ffn1_w_gate, ffn1_w_up, ffn1_w_down, mix_norm, w_in, fox_b_f,
           rwkv_mu, rwkv_w0, rwkv_w2, rwkv_a0, rwkv_a2, rwkv_g2, rwkv_k_k, rwkv_k_a, rwkv_r_k,
           rwkv_ln_g, rwkv_ln_b, w_o_fox, w_o_rwkv, w_out,
           ffn2_norm, ffn2_w_gate, ffn2_w_up, ffn2_w_down, ple_norm, ple_w_gate, ple_w_proj,
           final_norm):
    depth = w_in.shape[0]
    n_pr, s_len, d_model = x_prompt.shape
    n_dec = x_sample.shape[0]
    assert n_pr == 1 and x_sample.shape[1] == 1
    m_rows = s_len + n_dec
    n_pages = page_table.shape[1]
    pps = DECODE_PAGES_PER_STEP if n_pages % DECODE_PAGES_PER_STEP == 0 else 1
    tm = 640 if m_rows % 640 == 0 else LANES
    tm_wide = 1664 if m_rows % 1664 == 0 else tm
    tq = min(512, s_len)
    t_rwkv = n_dec

    off_f = 3 * C_FOX
    off_r = off_f + H_FOX
    off_g = off_r + N_RWKV
    qkv_scale = jnp.concatenate([jnp.full((1, C_FOX), LOG2E * DH_FOX ** -0.5, F32), jnp.ones((1, 2 * C_FOX), F32)], axis=1)
    sel = _head_selectors()
    row2 = lambda z: z.reshape(1, -1).astype(F32)

    x = jnp.concatenate([x_prompt[0], x_sample[:, 0]], axis=0)
    outs = {name: [] for name in ("kp", "vp", "lfp", "wkvp", "shp", "ks", "vs", "lfs", "wkvs", "shs")}
    ffn1_w = [_to_bf16(w) for w in (ffn1_w_gate, ffn1_w_up, ffn1_w_down)]
    ffn2_w = [_to_bf16(w) for w in (ffn2_w_gate, ffn2_w_up, ffn2_w_down)]
    mix_w = [_to_bf16(w) for w in (w_o_fox, w_o_rwkv, w_out)]
    ple_w = [_to_bf16(w) for w in (ple_w_gate, ple_w_proj)]
    w_in_bf = _to_bf16(w_in)
    cache_lf_t = jnp.swapaxes(cache_logf, 2, 3)
    for i in range(depth):
        w_i = w_in_bf[i]
        w_qkv = w_i[:, :off_f]
        w_f = _pad_cols(w_i[:, off_f:off_r], LANES)
        w_r = _repack_rwkv_cols(w_i[:, off_r:off_g])
        w_g = w_i[:, off_g:]
        b_f = _pad_cols(row2(fox_b_f[i]), LANES)

        x, h = _ffn(i, x, row2(ffn1_norm[i]), row2(mix_norm[i]), *ffn1_w, tm=tm, tf=512)

        qkv, qkv_bf = _proj(h, w_qkv, qkv_scale, mode="qkv", tm=tm_wide, tn=512)
        logf = _proj(h, w_f, b_f, mode="logf", tm=tm_wide, tn=LANES)[0][:, :H_FOX]
        proj_r = _proj(h, w_r, jnp.zeros((1, RP_W), F32), mode="plain", tm=tm_wide, tn=512)[0]
        gates = _proj(h, w_g, jnp.zeros((1, 2 * d_model), F32), mode="sigmoid", tm=tm_wide, tn=1024)[0]

        lf_p = logf[:s_len]
        c_row = _cumsum_rows(lf_p.T)
        o_fox_p = _fox_prompt(qkv_bf, c_row.T, s_len=s_len, tq=tq, tk=tq)
        q_s = qkv[s_len:, :C_FOX].reshape(n_dec, H_FOX, DH_FOX)
        k_s = qkv[s_len:, C_FOX:2 * C_FOX].reshape(n_dec, H_FOX, DH_FOX)
        v_s = qkv[s_len:, 2 * C_FOX:].reshape(n_dec, H_FOX, DH_FOX)
        lf_s = logf[s_len:]
        o_fox_s = _fox_decode(
            i, page_table, q_s, k_s, v_s, jnp.tile(lf_s, (1, LANES // H_FOX)).reshape(n_dec, 1, LANES),
            cache_k, cache_v, cache_lf_t, pps=pps)
        o_fox = jnp.concatenate([o_fox_p, o_fox_s.reshape(n_dec, C_FOX).astype(BF16)], axis=0)

        vecs = (_repack_rwkv_cols(row2(rwkv_mu[i])), row2(rwkv_w0[i]), row2(rwkv_a0[i]),
                row2(rwkv_k_k[i]), row2(rwkv_k_a[i]), row2(rwkv_r_k[i]))
        mats = (_pad_rows(rwkv_w2[i], LANES).astype(BF16), _pad_rows(rwkv_a2[i], LANES).astype(BF16),
                _pad_rows(rwkv_g2[i], 2 * LANES).astype(BF16))
        prev_s = _repack_rwkv_cols(state_shift[i].astype(F32))
        pre = _rwkv_prep(proj_r, prev_s, vecs, mats, sel, s_len=s_len, tm=t_rwkv)
        y_p, wkv_p = _rwkv_scan(*pre[:6], t=s_len)
        y_s, wkv_s = _rwkv_step(*[z[s_len:] for z in pre[:6]], state_wkv[i].astype(F32), nb=8)
        o_rwkv = _rwkv_post(y_p, y_s, pre[6], pre[7], row2(rwkv_ln_g[i]), row2(rwkv_ln_b[i]), sel, tm=t_rwkv)

        x = _merge(i, x, o_fox, o_rwkv, gates, *mix_w, tm=tm, tn=512)
        x, u = _ffn(i, x, row2(ffn2_norm[i]), row2(ple_norm[i]), *ffn2_w, tm=tm, tf=512)
        p_all = jnp.concatenate([p_prompt[i, 0], p_sample[i, :, 0]], axis=0)
        x = _ple(i, x, u, p_all, *ple_w, tm=tm_wide, tn=512)

        outs["kp"].append(qkv[:s_len, C_FOX:2 * C_FOX].reshape(1, s_len, H_FOX, DH_FOX))
        outs["vp"].append(qkv[:s_len, 2 * C_FOX:].reshape(1, s_len, H_FOX, DH_FOX))
        outs["lfp"].append(lf_p.reshape(1, s_len, H_FOX))
        outs["wkvp"].append(wkv_p[None])
        outs["shp"].append(_unpack_rwkv_cols(proj_r[s_len - 1:s_len]))
        outs["ks"].append(k_s.reshape(n_dec, 1, H_FOX, DH_FOX))
        outs["vs"].append(v_s.reshape(n_dec, 1, H_FOX, DH_FOX))
        outs["lfs"].append(lf_s.reshape(n_dec, 1, H_FOX))
        outs["wkvs"].append(wkv_s)
        outs["shs"].append(_unpack_rwkv_cols(proj_r[s_len:]))

    y = _final_norm(x, row2(final_norm), tm=tm)
    st = {k: jnp.stack(v) for k, v in outs.items()}
    return (y[:s_len].reshape(1, s_len, d_model), y[s_len:].reshape(n_dec, 1, d_model),
            st["kp"], st["vp"], st["lfp"], st["wkvp"], st["shp"],
            st["ks"], st["vs"], st["lfs"], st["wkvs"], st["shs"])
```

```python
import functools

import jax
import jax.numpy as jnp
from jax import lax
from jax.experimental import pallas as pl
from jax.experimental.pallas import tpu as pltpu

F32 = jnp.float32
BF16 = jnp.bfloat16

NORM_EPS = 1e-6
LN_X_EPS = 64e-5
KK_EPS = 1e-12

LANES = 128
VMEM_LIMIT = 56 * 1024 * 1024

H_FOX = 8
DH_FOX = 128
C_FOX = H_FOX * DH_FOX
H_RWKV = 16
DH_RWKV = 64
C_RWKV = H_RWKV * DH_RWKV
W_LORA = 64
A_LORA = 64
G_LORA = 160
N_RWKV = 3 * C_RWKV + W_LORA + A_LORA + G_LORA
RP_WL = 3 * C_RWKV
RP_AL = RP_WL + LANES
RP_GL = RP_AL + LANES
RP_W = RP_GL + 2 * LANES
LOG2E = 1.4426950408889634
DECODE_PAGES_PER_STEP = 16
SCAN_CHUNK = 64


def _cparams(sem):
    return pltpu.CompilerParams(dimension_semantics=sem, vmem_limit_bytes=VMEM_LIMIT)


def _sigmoid(x):
    return 1.0 / (1.0 + jnp.exp(-x))


def _softplus(x):
    return jnp.maximum(x, 0.0) + jnp.log(1.0 + jnp.exp(-jnp.abs(x)))


def _rms_rows(x, g):
    ms = jnp.mean(x * x, axis=-1, keepdims=True)
    return x * lax.rsqrt(ms + NORM_EPS) * g


def _dot(a, b):
    return jnp.dot(a, b, preferred_element_type=F32)


def _split2(x):
    hi = x.astype(BF16)
    lo = (x - hi.astype(F32)).astype(BF16)
    return hi, lo


def _ffn_kernel(x_ref, g_ref, gn_ref, wg_ref, wu_ref, wd_ref, o_ref, hn_ref, h_scr, acc_scr):
    f = pl.program_id(1)

    @pl.when(f == 0)
    def _():
        h_scr[...] = _rms_rows(x_ref[...], g_ref[...]).astype(BF16)
        acc_scr[...] = jnp.zeros_like(acc_scr)

    h = h_scr[...]
    gate = _dot(h, wg_ref[...])
    up = _dot(h, wu_ref[...])
    act = (gate * _sigmoid(gate) * up).astype(BF16)
    acc_scr[...] += _dot(act, wd_ref[...])

    @pl.when(f == pl.num_programs(1) - 1)
    def _():
        out = x_ref[...] + 0.5 * acc_scr[...]
        o_ref[...] = out
        hn_ref[...] = _rms_rows(out, gn_ref[...]).astype(BF16)


def _cast_kernel(w_ref, o_ref):
    o_ref[...] = w_ref[...].astype(o_ref.dtype)


def _to_bf16(w):
    depth, rows, cols = w.shape
    tr = next(t for t in (512, 256, 128, 64, 32, 16) if rows % t == 0 and t * cols * 4 <= 6 * 2 ** 20)
    spec = pl.BlockSpec((None, tr, cols), lambda l, r: (l, r, 0))
    return pl.pallas_call(
        _cast_kernel,
        grid=(depth, rows // tr),
        in_specs=[spec],
        out_specs=spec,
        out_shape=jax.ShapeDtypeStruct(w.shape, BF16),
        compiler_params=_cparams(("parallel", "parallel")),
        name="to_bf16",
    )(w)


def _ffn(layer, x, g, g_next, wg, wu, wd, *, tm, tf):
    m, d = x.shape
    dff = wg.shape[2]
    row = pl.BlockSpec((tm, d), lambda i, f: (i, 0))
    vec = pl.BlockSpec((1, d), lambda i, f: (0, 0))
    return pl.pallas_call(
        _ffn_kernel,
        grid=(m // tm, dff // tf),
        in_specs=[
            row, vec, vec,
            pl.BlockSpec((None, d, tf), lambda i, f: (layer, 0, f)),
            pl.BlockSpec((None, d, tf), lambda i, f: (layer, 0, f)),
            pl.BlockSpec((None, tf, d), lambda i, f: (layer, f, 0)),
        ],
        out_specs=[row, row],
        out_shape=[jax.ShapeDtypeStruct((m, d), F32), jax.ShapeDtypeStruct((m, d), BF16)],
        scratch_shapes=[pltpu.VMEM((tm, d), BF16), pltpu.VMEM((tm, d), F32)],
        compiler_params=_cparams(("parallel", "arbitrary")),
        name="ffn",
    )(x, g, g_next, wg, wu, wd)


def _proj_kernel(h_ref, w_ref, aux_ref, *outs, mode):
    acc = _dot(h_ref[...], w_ref[...])
    if mode == "qkv":
        outs[0][...] = acc
        outs[1][...] = (acc * aux_ref[...]).astype(BF16)
    elif mode == "logf":
        outs[0][...] = -_softplus(-(acc + aux_ref[...]))
    elif mode == "sigmoid":
        outs[0][...] = _sigmoid(acc)
    else:
        outs[0][...] = acc


def _proj(h, w, aux, *, mode, tm, tn):
    m, d = h.shape
    n = w.shape[1]
    out_shape = [jax.ShapeDtypeStruct((m, n), F32)]
    if mode == "qkv":
        out_shape.append(jax.ShapeDtypeStruct((m, n), BF16))
    o_spec = pl.BlockSpec((tm, tn), lambda i, j: (i, j))
    return pl.pallas_call(
        functools.partial(_proj_kernel, mode=mode),
        grid=(m // tm, n // tn),
        in_specs=[
            pl.BlockSpec((tm, d), lambda i, j: (i, 0)),
            pl.BlockSpec((d, tn), lambda i, j: (0, j)),
            pl.BlockSpec((1, tn), lambda i, j: (0, j)),
        ],
        out_specs=[o_spec] * len(out_shape),
        out_shape=out_shape,
        compiler_params=_cparams(("parallel", "arbitrary")),
        name="proj_" + mode,
    )(h, w, aux)


def _merge_kernel(x_ref, of_ref, or_ref, gf_ref, gr_ref, wof_ref, wor_ref, wout_ref, o_ref, acc_scr):
    n = pl.program_id(1)

    @pl.when(n == 0)
    def _():
        acc_scr[...] = jnp.zeros_like(acc_scr)

    merged = gf_ref[...] * _dot(of_ref[...], wof_ref[...]) + gr_ref[...] * _dot(or_ref[...], wor_ref[...])
    acc_scr[...] += _dot(merged.astype(BF16), wout_ref[...])

    @pl.when(n == pl.num_programs(1) - 1)
    def _():
        o_ref[...] = x_ref[...] + acc_scr[...]


def _merge(layer, x, o_fox, o_rwkv, gates, wof, wor, wout, *, tm, tn):
    m, d = x.shape
    nb = d // tn
    return pl.pallas_call(
        _merge_kernel,
        grid=(m // tm, nb),
        in_specs=[
            pl.BlockSpec((tm, d), lambda i, n: (i, 0)),
            pl.BlockSpec((tm, o_fox.shape[1]), lambda i, n: (i, 0)),
            pl.BlockSpec((tm, o_rwkv.shape[1]), lambda i, n: (i, 0)),
            pl.BlockSpec((tm, tn), lambda i, n: (i, n)),
            pl.BlockSpec((tm, tn), lambda i, n: (i, n + nb)),
            pl.BlockSpec((None, wof.shape[1], tn), lambda i, n: (layer, 0, n)),
            pl.BlockSpec((None, wor.shape[1], tn), lambda i, n: (layer, 0, n)),
            pl.BlockSpec((None, tn, d), lambda i, n: (layer, n, 0)),
        ],
        out_specs=pl.BlockSpec((tm, d), lambda i, n: (i, 0)),
        out_shape=jax.ShapeDtypeStruct((m, d), F32),
        scratch_shapes=[pltpu.VMEM((tm, d), F32)],
        compiler_params=_cparams(("parallel", "arbitrary")),
        name="merge",
    )(x, o_fox, o_rwkv, gates, gates, wof, wor, wout)


def _ple_kernel(x_ref, u_ref, p_ref, wg_ref, wp_ref, o_ref):
    gate = _sigmoid(_dot(u_ref[...], wg_ref[...]))
    emb = _dot(p_ref[...].astype(BF16), wp_ref[...])
    o_ref[...] = x_ref[...] + gate * emb


def _ple(layer, x, u, p, wg, wp, *, tm, tn):
    m, d = x.shape
    return pl.pallas_call(
        _ple_kernel,
        grid=(m // tm, d // tn),
        in_specs=[
            pl.BlockSpec((tm, tn), lambda i, j: (i, j)),
            pl.BlockSpec((tm, d), lambda i, j: (i, 0)),
            pl.BlockSpec((tm, p.shape[1]), lambda i, j: (i, 0)),
            pl.BlockSpec((None, d, tn), lambda i, j: (layer, 0, j)),
            pl.BlockSpec((None, p.shape[1], tn), lambda i, j: (layer, 0, j)),
        ],
        out_specs=pl.BlockSpec((tm, tn), lambda i, j: (i, j)),
        out_shape=jax.ShapeDtypeStruct((m, d), F32),
        compiler_params=_cparams(("parallel", "arbitrary")),
        name="ple",
    )(x, u, p, wg, wp)


def _norm_kernel(x_ref, g_ref, seq_ref, rows_ref, *, seq_tiles):
    y = _rms_rows(x_ref[...], g_ref[...])

    @pl.when(pl.program_id(0) < seq_tiles)
    def _():
        seq_ref[...] = y

    @pl.when(pl.program_id(0) >= seq_tiles)
    def _():
        rows_ref[...] = y


def _final_norm(x, g, *, s_len, tm):
    m, d = x.shape
    assert s_len % tm == 0 and (m - s_len) % tm == 0
    seq_tiles = s_len // tm
    return pl.pallas_call(
        functools.partial(_norm_kernel, seq_tiles=seq_tiles),
        grid=(m // tm,),
        in_specs=[pl.BlockSpec((tm, d), lambda i: (i, 0)), pl.BlockSpec((1, d), lambda i: (0, 0))],
        out_specs=[pl.BlockSpec((tm, d), lambda i: (jnp.minimum(i, seq_tiles - 1), 0)),
                   pl.BlockSpec((tm, d), lambda i: (jnp.maximum(i - seq_tiles, 0), 0))],
        out_shape=[jax.ShapeDtypeStruct((s_len, d), F32), jax.ShapeDtypeStruct((m - s_len, d), F32)],
        compiler_params=_cparams(("arbitrary",)),
        name="final_norm",
    )(x, g)


def _cumsum_kernel(lf_ref, o_ref):
    h, s = lf_ref.shape
    row = lax.broadcasted_iota(jnp.int32, (LANES, LANES), 0)
    col = lax.broadcasted_iota(jnp.int32, (LANES, LANES), 1)
    tri = jnp.where(row <= col, 1.0, 0.0)

    def body(c, carry):
        off = pl.multiple_of(c * LANES, LANES)
        y = jnp.dot(lf_ref[:, pl.ds(off, LANES)], tri, precision=lax.Precision.HIGHEST,
                    preferred_element_type=F32) + carry
        o_ref[:, pl.ds(off, LANES)] = y
        return y[:, LANES - 1:LANES]

    lax.fori_loop(0, s // LANES, body, jnp.zeros((h, 1), F32))


def _cumsum_rows(lf_t):
    return pl.pallas_call(
        _cumsum_kernel,
        out_shape=jax.ShapeDtypeStruct(lf_t.shape, F32),
        compiler_params=pltpu.CompilerParams(vmem_limit_bytes=VMEM_LIMIT),
        name="logf_cumsum",
    )(lf_t)


def _fox_prompt_kernel(q_ref, k_ref, v_ref, qx_ref, kx_ref, o_ref, m_scr, l_scr, acc_scr, *, tq, tk):
    qi = pl.program_id(0)
    kj = pl.program_id(1)

    @pl.when(kj == 0)
    def _():
        m_scr[...] = jnp.full_like(m_scr, -jnp.inf)
        l_scr[...] = jnp.zeros_like(l_scr)
        acc_scr[...] = jnp.zeros_like(acc_scr)

    def update(on_diagonal):
        if on_diagonal:
            causal = (lax.broadcasted_iota(jnp.int32, (tq, tk), 1)
                      <= lax.broadcasted_iota(jnp.int32, (tq, tk), 0))
        heads = range(H_FOX)
        sl = [slice(h * DH_FOX, (h + 1) * DH_FOX) for h in heads]
        ones = jnp.ones((tk, LANES), BF16)
        t = []
        for h in heads:
            q_wide = jnp.concatenate([q_ref[:, sl[h]], qx_ref[h]], axis=1)
            k_wide = jnp.concatenate([k_ref[:, sl[h]], kx_ref[h]], axis=1)
            th = lax.dot_general(q_wide, k_wide, (_NT, ((), ())), preferred_element_type=F32)
            t.append(jnp.where(causal, th, -jnp.inf) if on_diagonal else th)
        m_old = [m_scr[h] for h in heads]
        m_new = [jnp.maximum(m_old[h], jnp.max(t[h], axis=-1, keepdims=True)) for h in heads]
        alpha = [jnp.exp2(m_old[h] - m_new[h]) for h in heads]
        p = [jnp.exp2(t[h] - jnp.tile(m_new[h], (1, tk // LANES))).astype(BF16) for h in heads]
        for h in heads:
            pv = _dot(p[h], jnp.concatenate([v_ref[:, sl[h]], ones], axis=1))
            acc_scr[h] = alpha[h] * acc_scr[h] + pv[:, :DH_FOX]
            l_scr[h] = alpha[h] * l_scr[h] + pv[:, DH_FOX:]
            m_scr[h] = m_new[h]

    @pl.when(kj < qi)
    def _():
        update(False)

    @pl.when(kj == qi)
    def _():
        update(True)
        for h in range(H_FOX):
            o_ref[:, h * DH_FOX:(h + 1) * DH_FOX] = (acc_scr[h] / l_scr[h]).astype(o_ref.dtype)


def _forget_blocks(c_col):
    def top16(x):
        bits = lax.bitcast_convert_type(x, jnp.uint32) & jnp.uint32(0xFFFF0000)
        return lax.bitcast_convert_type(bits, F32)

    c2 = (c_col * LOG2E).T[:, :, None]
    hi = top16(c2)
    mid = top16(c2 - hi)
    lo = c2 - hi - mid
    lane = lax.broadcasted_iota(jnp.int32, c2.shape[:2] + (LANES,), 2)
    part = lambda j: jnp.where(lane == j, hi, jnp.where(lane == j + 1, mid, jnp.where(lane == j + 2, lo, 0.0)))
    one = lambda j: jnp.where((lane >= j) & (lane < j + 3), 1.0, 0.0)
    qx = (part(0) + one(3)).astype(BF16)
    kx = (one(0) - part(3)).astype(BF16)
    return qx, kx


def _fox_prompt(qkv_bf, c_col, *, s_len, tq, tk):
    assert tq == tk and DH_FOX == LANES
    nq, nk = s_len // tq, s_len // tk
    qx, kx = _forget_blocks(c_col)
    return pl.pallas_call(
        functools.partial(_fox_prompt_kernel, tq=tq, tk=tk),
        grid=(nq, nk),
        in_specs=[
            pl.BlockSpec((tq, C_FOX), lambda i, j: (i, 0)),
            pl.BlockSpec((tk, C_FOX), lambda i, j: (jnp.minimum(i, j), 1)),
            pl.BlockSpec((tk, C_FOX), lambda i, j: (jnp.minimum(i, j), 2)),
            pl.BlockSpec((H_FOX, tq, LANES), lambda i, j: (0, i, 0)),
            pl.BlockSpec((H_FOX, tk, LANES), lambda i, j: (0, jnp.minimum(i, j), 0)),
        ],
        out_specs=pl.BlockSpec((tq, C_FOX), lambda i, j: (i, 0)),
        out_shape=jax.ShapeDtypeStruct((s_len, C_FOX), BF16),
        scratch_shapes=[
            pltpu.VMEM((H_FOX, tq, LANES), F32),
            pltpu.VMEM((H_FOX, tq, LANES), F32),
            pltpu.VMEM((H_FOX, tq, DH_FOX), F32),
        ],
        compiler_params=_cparams(("parallel", "arbitrary")),
        name="fox_prompt",
    )(qkv_bf, qkv_bf, qkv_bf, qx, kx)


def _fox_decode_kernel(pt_ref, q_ref, kn_ref, vn_ref, lfn_ref, sfx_ref, *rest, scale, pps):
    del pt_ref
    kc, vc, lfc = rest[:pps], rest[pps:2 * pps], rest[2 * pps:3 * pps]
    o_ref, m_scr, l_scr, acc_scr, carry_scr = rest[3 * pps:]
    p = pl.program_id(1)
    page = kc[0].shape[0]
    nrow = page * H_FOX
    sub = lax.broadcasted_iota(jnp.int32, (H_FOX, nrow + LANES), 0)
    lane = lax.broadcasted_iota(jnp.int32, (H_FOX, nrow + LANES), 1)
    own_ext = (lane % H_FOX) == sub
    own = own_ext[:, :nrow]
    pick = lane[:, :LANES] == sub[:, :LANES]
    q8 = q_ref[...] * scale

    def to_col(row):
        return jnp.sum(jnp.where(pick, row, 0.0), axis=-1, keepdims=True)

    def per_head(row, op):
        parts = [row[:, j * LANES:(j + 1) * LANES] for j in range(nrow // LANES)]
        while len(parts) > 1:
            parts = [op(parts[i], parts[i + 1]) for i in range(0, len(parts), 2)]
        acc = parts[0]
        shift = H_FOX
        while shift < LANES:
            acc = op(acc, pltpu.roll(acc, shift, axis=1))
            shift *= 2
        return acc

    @pl.when(p == 0)
    def _():
        m_scr[...] = jnp.full_like(m_scr, -jnp.inf)
        l_scr[...] = jnp.zeros_like(l_scr)
        acc_scr[...] = jnp.zeros_like(acc_scr)
        carry_scr[...] = lfn_ref[...]

    q8_bf = q8.astype(BF16)
    widen = lambda row: jnp.tile(row, (1, nrow // LANES))
    carry = carry_scr[...]
    scores = []
    for j in range(pps):
        k2 = kc[j][...].reshape(nrow, DH_FOX).astype(BF16)
        g = lax.dot_general(q8_bf, k2, (_NT, ((), ())), preferred_element_type=F32)
        s = jnp.sum(jnp.where(own, g, 0.0), axis=0, keepdims=True)

        lf = lfc[j][...]
        hi = lf.astype(BF16).astype(F32)
        lf2 = jnp.concatenate([hi, lf - hi], axis=1).astype(BF16)
        t = _dot(lf2, sfx_ref[...])
        bias = jnp.sum(jnp.where(own_ext, t, 0.0), axis=0, keepdims=True)
        scores.append(s + bias[:, :nrow] + widen(carry))
        carry = carry + bias[:, nrow:]
    carry_scr[...] = carry

    m_old = m_scr[...]
    m_new = m_old
    for s in scores:
        m_new = jnp.maximum(m_new, per_head(s, jnp.maximum))
    alpha = jnp.exp(m_old - m_new)
    m_wide = widen(m_new)
    l_new = alpha * l_scr[...]
    acc = to_col(alpha) * acc_scr[...]
    for j, s in enumerate(scores):
        pr = jnp.exp(s - m_wide)
        l_new = l_new + per_head(pr, jnp.add)
        p8 = jnp.where(own, pr, 0.0).astype(BF16)
        acc = acc + _dot(p8, vc[j][...].reshape(nrow, DH_FOX).astype(BF16))
    l_scr[...] = l_new
    acc_scr[...] = acc
    m_scr[...] = m_new

    @pl.when(p == pl.num_programs(1) - 1)
    def _():
        s_new = jnp.sum(q8 * kn_ref[...], axis=-1, keepdims=True)
        m_old = to_col(m_scr[...])
        m_fin = jnp.maximum(m_old, s_new)
        alpha = jnp.exp(m_old - m_fin)
        p_new = jnp.exp(s_new - m_fin)
        l_fin = alpha * to_col(l_scr[...]) + p_new
        o_ref[...] = (alpha * acc_scr[...] + p_new * vn_ref[...]) / l_fin


def _suffix_selector(page):
    r = jnp.arange(page * H_FOX + LANES)
    tok = jnp.arange(page)
    later = jnp.where(r[None, :] < page * H_FOX, tok[:, None] > r[None, :] // H_FOX, True)
    return jnp.tile(later.astype(BF16), (2, 1))


def _fox_decode(layer, page_table, q, k_new, v_new, lf_new, cache_k, cache_v, cache_lf, *, pps):
    n, n_pages = page_table.shape
    page = cache_k.shape[2]
    pt_flat = page_table.reshape(-1)
    sfx = _suffix_selector(page)

    def page_map(j, tail):
        def index_map(b, p, pt):
            return (layer, pt[b * n_pages + (n_pages - 1 - (p * pps + j))]) + tail
        return index_map

    row_spec = pl.BlockSpec((None, H_FOX, DH_FOX), lambda b, p, pt: (b, 0, 0))
    kv_specs = [pl.BlockSpec((None, None, page, H_FOX, DH_FOX), page_map(j, (0, 0, 0))) for j in range(pps)]
    lf_specs = [pl.BlockSpec((None, None, H_FOX, page), page_map(j, (0, 0))) for j in range(pps)]
    grid_spec = pltpu.PrefetchScalarGridSpec(
        num_scalar_prefetch=1,
        grid=(n, n_pages // pps),
        in_specs=[
            row_spec, row_spec, row_spec,
            pl.BlockSpec((None, 1, LANES), lambda b, p, pt: (b, 0, 0)),
            pl.BlockSpec(sfx.shape, lambda b, p, pt: (0, 0)),
        ] + kv_specs + kv_specs + lf_specs,
        out_specs=row_spec,
        scratch_shapes=[
            pltpu.VMEM((1, LANES), F32),
            pltpu.VMEM((1, LANES), F32),
            pltpu.VMEM((H_FOX, DH_FOX), F32),
            pltpu.VMEM((1, LANES), F32),
        ],
    )
    return pl.pallas_call(
        functools.partial(_fox_decode_kernel, scale=DH_FOX ** -0.5, pps=pps),
        grid_spec=grid_spec,
        out_shape=jax.ShapeDtypeStruct((n, H_FOX, DH_FOX), F32),
        compiler_params=_cparams(("parallel", "arbitrary")),
        name="fox_decode",
    )(pt_flat, q, k_new, v_new, lf_new, sfx, *([cache_k] * pps), *([cache_v] * pps), *([cache_lf] * pps))


def _seg_sum(x, e_ref, et_ref):
    hi, lo = _split2(x)
    s = _dot(hi, e_ref[...]) + _dot(lo, e_ref[...])
    shi, slo = _split2(s)
    return _dot(shi, et_ref[...]) + _dot(slo, et_ref[...])


def _rwkv_prep_kernel(pr_ref, pv_ref, ps_ref, mu_ref, w0_ref, a0_ref, kk_ref, ka_ref, rk_ref,
                      w2_ref, a2_ref, g2_ref, e_ref, et_ref,
                      r_out, lw_out, k_out, v_out, a_out, b_out, g_out, bonus_out, *, seq_tiles):
    i = pl.program_id(0)
    pr = pr_ref[...]
    first = jnp.where(i == 0, 0.0, pv_ref[7:8, :])
    prev = jnp.where(lax.broadcasted_iota(jnp.int32, pr.shape, 0) == 0, first, pltpu.roll(pr, 1, axis=0))
    prev = jnp.where(i < seq_tiles, prev, ps_ref[...])
    xm = pr + (prev - pr) * mu_ref[...]
    r = xm[:, 0:C_RWKV]
    k = xm[:, C_RWKV:2 * C_RWKV]
    v = xm[:, 2 * C_RWKV:3 * C_RWKV]
    wl = xm[:, RP_WL:RP_AL]
    al = xm[:, RP_AL:RP_GL]
    gl = xm[:, RP_GL:RP_W]

    w_log = -_softplus(-(w0_ref[...] + _dot(jnp.tanh(wl).astype(BF16), w2_ref[...]))) - 0.5
    a_sig = _sigmoid(a0_ref[...] + _dot(al.astype(BF16), a2_ref[...]))
    g = _dot(_sigmoid(gl).astype(BF16), g2_ref[...])

    kk = k * kk_ref[...]
    norm = jnp.sqrt(_seg_sum(kk * kk, e_ref, et_ref))
    kk = kk / jnp.maximum(norm, KK_EPS)
    k_mod = k * (1.0 + (a_sig - 1.0) * ka_ref[...])

    r_out[...] = r
    lw_out[...] = -jnp.exp(w_log)
    k_out[...] = k_mod
    v_out[...] = v
    a_out[...] = -kk
    b_out[...] = kk * a_sig
    g_out[...] = g
    bonus_out[...] = _seg_sum(r * k_mod * rk_ref[...], e_ref, et_ref) * v


def _rwkv_prep(proj, prev_rows, vecs, mats, sel, *, s_len, tm):
    m = proj.shape[0]
    assert s_len % tm == 0 and (m - s_len) % tm == 0
    seq_tiles = s_len // tm
    mu, w0, a0, k_k, k_a, r_k = vecs
    w2, a2, g2 = mats
    e, et = sel
    row = lambda i: (i, 0)
    fix = lambda i: (0, 0)
    vec_spec = pl.BlockSpec((1, C_RWKV), fix)
    out = jax.ShapeDtypeStruct((m, C_RWKV), F32)
    return pl.pallas_call(
        functools.partial(_rwkv_prep_kernel, seq_tiles=seq_tiles),
        grid=(m // tm,),
        in_specs=[
            pl.BlockSpec((tm, RP_W), row),
            pl.BlockSpec((8, RP_W), lambda i: (jnp.maximum(i * (tm // 8) - 1, 0), 0)),
            pl.BlockSpec((tm, RP_W), lambda i: (jnp.maximum(i - seq_tiles, 0), 0)),
            pl.BlockSpec((1, RP_W), fix),
            vec_spec, vec_spec, vec_spec, vec_spec, vec_spec,
            pl.BlockSpec(w2.shape, fix), pl.BlockSpec(a2.shape, fix), pl.BlockSpec(g2.shape, fix),
            pl.BlockSpec(e.shape, fix), pl.BlockSpec(et.shape, fix),
        ],
        out_specs=[pl.BlockSpec((tm, C_RWKV), row)] * 8,
        out_shape=[out] * 8,
        compiler_params=_cparams(("parallel",)),
        name="rwkv_prep",
    )(proj, proj, prev_rows, mu, w0, a0, k_k, k_a, r_k, w2, a2, g2, e, et)


_NN = ((1,), (0,))
_NT = ((1,), (1,))
_TN = ((0,), (0,))


def _mm(a, b, dims):
    return lax.dot_general(a, b, (dims, ((), ())), preferred_element_type=F32)


def _rwkv_scan_kernel(r_ref, lw_ref, k_ref, v_ref, a_ref, b_ref, y_ref, s_out, s_scr):
    c = pl.program_id(0)
    ch = r_ref.shape[0]
    heads = range(H_RWKV)

    @pl.when(c == 0)
    def _():
        s_scr[...] = jnp.zeros_like(s_scr)

    row = lax.broadcasted_iota(jnp.int32, (ch, ch), 0)
    col = lax.broadcasted_iota(jnp.int32, (ch, ch), 1)
    incl = row >= col
    strict = row > col

    def cut(x):
        return [x[:, h * DH_RWKV:(h + 1) * DH_RWKV].astype(BF16) for h in heads]

    lw = lw_ref[...]
    cs = jnp.dot(jnp.where(incl, 1.0, 0.0), lw, precision=lax.Precision.HIGHEST, preferred_element_type=F32)
    p_incl = jnp.exp(cs)
    p_inv = jnp.exp(-cs)
    p_excl = jnp.exp(cs - lw)
    rm = cut(r_ref[...] * p_incl)
    am = cut(a_ref[...] * p_excl)
    bm = cut(b_ref[...] * p_inv)
    km = cut(k_ref[...] * p_inv)
    vm = cut(v_ref[...])
    p_end = p_incl[ch - 1:ch, :]
    s0 = [s_scr[h] for h in heads]
    s0b = [s.astype(BF16) for s in s0]

    def masked(xs, ys, keep):
        return [jnp.where(keep, _mm(x, y, _NT), 0.0).astype(BF16) for x, y in zip(xs, ys)]

    l_ab = masked(am, bm, strict)
    l_ak = masked(am, km, strict)
    m_rb = masked(rm, bm, incl)
    m_rk = masked(rm, km, incl)
    u = [_mm(am[h], s0b[h], _NT) + _mm(l_ak[h], vm[h], _NN) for h in heads]
    y = [_mm(rm[h], s0b[h], _NT) + _mm(m_rk[h], vm[h], _NN) for h in heads]
    pw = l_ab
    u = [u[h] + _mm(pw[h], u[h].astype(BF16), _NN) for h in heads]
    span = 2
    while span < ch:
        pw = [_mm(pw[h], pw[h], _NN).astype(BF16) for h in heads]
        u = [u[h] + _mm(pw[h], u[h].astype(BF16), _NN) for h in heads]
        span *= 2
    ub = [x.astype(BF16) for x in u]
    y = [y[h] + _mm(m_rb[h], ub[h], _NN) for h in heads]
    y_ref[...] = jnp.concatenate(y, axis=1)
    for h in heads:
        s_new = s0[h] + _mm(ub[h], bm[h], _TN) + _mm(vm[h], km[h], _TN)
        s_scr[h] = s_new * p_end[:, h * DH_RWKV:(h + 1) * DH_RWKV]

    @pl.when(c == pl.num_programs(0) - 1)
    def _():
        s_out[...] = s_scr[...]


def _rwkv_scan(r, lw, k, v, a, b, *, t):
    spec = pl.BlockSpec((SCAN_CHUNK, C_RWKV), lambda c: (c, 0))
    st = (H_RWKV, DH_RWKV, DH_RWKV)
    return pl.pallas_call(
        _rwkv_scan_kernel,
        grid=(t // SCAN_CHUNK,),
        in_specs=[spec] * 6,
        out_specs=[spec, pl.BlockSpec(st, lambda c: (0, 0, 0))],
        out_shape=[jax.ShapeDtypeStruct((t, C_RWKV), F32), jax.ShapeDtypeStruct(st, F32)],
        scratch_shapes=[pltpu.VMEM(st, F32)],
        compiler_params=_cparams(("arbitrary",)),
        name="rwkv_scan",
    )(r, lw, k, v, a, b)


def _rwkv_step_kernel(r_ref, lw_ref, k_ref, v_ref, a_ref, b_ref, s_ref, y_ref, s_out):
    s = s_ref[...]
    dv, dk = s.shape[-2:]
    eye = (lax.broadcasted_iota(jnp.int32, (dv, dk), 0) == lax.broadcasted_iota(jnp.int32, (dv, dk), 1))
    sa = jnp.sum(s * a_ref[...], axis=-1, keepdims=True)
    v_col = jnp.sum(jnp.where(eye, v_ref[...], 0.0), axis=-1, keepdims=True)
    s_new = s * jnp.exp(lw_ref[...]) + sa * b_ref[...] + v_col * k_ref[...]
    y_col = jnp.sum(s_new * r_ref[...], axis=-1, keepdims=True)
    y_ref[...] = jnp.sum(jnp.where(eye, y_col, 0.0), axis=-2, keepdims=True)
    s_out[...] = s_new


def _rwkv_step(layer, r, lw, k, v, a, b, state, *, nb):
    n = state.shape[1]
    vec = lambda z: z.reshape(n, H_RWKV, 1, DH_RWKV)
    vspec = pl.BlockSpec((nb, H_RWKV, 1, DH_RWKV), lambda i: (i, 0, 0, 0))
    sspec = pl.BlockSpec((nb, H_RWKV, DH_RWKV, DH_RWKV), lambda i: (i, 0, 0, 0))
    sspec_in = pl.BlockSpec((None, nb, H_RWKV, DH_RWKV, DH_RWKV), lambda i: (layer, i, 0, 0, 0))
    y, s_new = pl.pallas_call(
        _rwkv_step_kernel,
        grid=(n // nb,),
        in_specs=[vspec] * 6 + [sspec_in],
        out_specs=[vspec, sspec],
        out_shape=[jax.ShapeDtypeStruct((n, H_RWKV, 1, DH_RWKV), F32),
                   jax.ShapeDtypeStruct(state.shape[1:], F32)],
        compiler_params=_cparams(("parallel",)),
        name="rwkv_step",
    )(vec(r), vec(lw), vec(k), vec(v), vec(a), vec(b), state)
    return y.reshape(n, C_RWKV), s_new


def _rwkv_post_kernel(yp_ref, ys_ref, g_ref, bonus_ref, lng_ref, lnb_ref, e_ref, et_ref, o_ref, *, seq_tiles):
    y = jnp.where(pl.program_id(0) < seq_tiles, yp_ref[...], ys_ref[...])
    inv_n = 1.0 / DH_RWKV
    mean = _seg_sum(y, e_ref, et_ref) * inv_n
    yc = y - mean
    var = _seg_sum(yc * yc, e_ref, et_ref) * inv_n
    yn = yc * lax.rsqrt(var + LN_X_EPS) * lng_ref[...] + lnb_ref[...]
    o_ref[...] = ((yn + bonus_ref[...]) * g_ref[...]).astype(o_ref.dtype)


def _rwkv_post(y_seq, y_rows, g, bonus, ln_g, ln_b, sel, *, tm):
    m = g.shape[0]
    s_len = y_seq.shape[0]
    assert s_len % tm == 0 and y_rows.shape[0] % tm == 0
    seq_tiles = s_len // tm
    e, et = sel
    row = pl.BlockSpec((tm, C_RWKV), lambda i: (i, 0))
    vec = pl.BlockSpec((1, C_RWKV), lambda i: (0, 0))
    return pl.pallas_call(
        functools.partial(_rwkv_post_kernel, seq_tiles=seq_tiles),
        grid=(m // tm,),
        in_specs=[pl.BlockSpec((tm, C_RWKV), lambda i: (jnp.minimum(i, seq_tiles - 1), 0)),
                  pl.BlockSpec((tm, C_RWKV), lambda i: (jnp.maximum(i - seq_tiles, 0), 0)),
                  row, row, vec, vec,
                  pl.BlockSpec(e.shape, lambda i: (0, 0)), pl.BlockSpec(et.shape, lambda i: (0, 0))],
        out_specs=row,
        out_shape=jax.ShapeDtypeStruct((m, C_RWKV), BF16),
        compiler_params=_cparams(("parallel",)),
        name="rwkv_post",
    )(y_seq, y_rows, g, bonus, ln_g, ln_b, e, et)


def _pad_cols(w, width):
    return jnp.pad(w, ((0, 0), (0, width - w.shape[1])))


def _pad_rows(w, height):
    return jnp.pad(w, ((0, height - w.shape[0]), (0, 0)))


def _repack_rwkv_cols(w):
    wl0 = 3 * C_RWKV
    al0 = wl0 + W_LORA
    gl0 = al0 + A_LORA
    return jnp.concatenate([
        w[:, :wl0],
        _pad_cols(w[:, wl0:al0], LANES),
        _pad_cols(w[:, al0:gl0], LANES),
        _pad_cols(w[:, gl0:], 2 * LANES),
    ], axis=1)


def _unpack_rwkv_cols(w):
    return jnp.concatenate([
        w[:, :RP_WL], w[:, RP_WL:RP_WL + W_LORA], w[:, RP_AL:RP_AL + A_LORA], w[:, RP_GL:RP_GL + G_LORA],
    ], axis=1)


def _head_selectors():
    lane_head = jnp.arange(C_RWKV) // DH_RWKV
    e = (lane_head[:, None] == jnp.arange(LANES)[None, :]).astype(BF16)
    return e, e.T


def kernel(x_prompt, x_sample, cache_k, cache_v, cache_logf, state_wkv, state_shift, page_table,
           p_prompt, p_sample,
           ffn1_norm, ffn1_w_gate, ffn1_w_up, ffn1_w_down, mix_norm, w_in, fox_b_f,
           rwkv_mu, rwkv_w0, rwkv_w2, rwkv_a0, rwkv_a2, rwkv_g2, rwkv_k_k, rwkv_k_a, rwkv_r_k,
           rwkv_ln_g, rwkv_ln_b, w_o_fox, w_o_rwkv, w_out,
           ffn2_norm, ffn2_w_gate, ffn2_w_up, ffn2_w_down, ple_norm, ple_w_gate, ple_w_proj,
           final_norm):
    depth = w_in.shape[0]
    n_pr, s_len, d_model = x_prompt.shape
    n_dec = x_sample.shape[0]
    assert n_pr == 1 and x_sample.shape[1] == 1
    m_rows = s_len + n_dec
    n_pages = page_table.shape[1]
    pps = DECODE_PAGES_PER_STEP if n_pages % DECODE_PAGES_PER_STEP == 0 else 1
    tm = 640 if m_rows % 640 == 0 else LANES
    tm_wide = 1664 if m_rows % 1664 == 0 else tm
    tq = min(512, s_len)
    t_rwkv = n_dec

    off_f = 3 * C_FOX
    off_r = off_f + H_FOX
    off_g = off_r + N_RWKV
    qkv_scale = jnp.concatenate([jnp.full((1, C_FOX), LOG2E * DH_FOX ** -0.5, F32), jnp.ones((1, 2 * C_FOX), F32)], axis=1)
    sel = _head_selectors()
    row2 = lambda z: z.reshape(1, -1).astype(F32)

    x = jnp.concatenate([x_prompt[0], x_sample[:, 0]], axis=0)
    outs = {name: [] for name in ("kp", "vp", "lfp", "wkvp", "shp", "ks", "vs", "lfs", "wkvs", "shs")}
    ffn1_w = [_to_bf16(w) for w in (ffn1_w_gate, ffn1_w_up, ffn1_w_down)]
    ffn2_w = [_to_bf16(w) for w in (ffn2_w_gate, ffn2_w_up, ffn2_w_down)]
    mix_w = [_to_bf16(w) for w in (w_o_fox, w_o_rwkv, w_out)]
    ple_w = [_to_bf16(w) for w in (ple_w_gate, ple_w_proj)]
    w_in_bf = _to_bf16(w_in)
    cache_lf_t = jnp.swapaxes(cache_logf, 2, 3)
    for i in range(depth):
        w_i = w_in_bf[i]
        w_qkv = w_i[:, :off_f]
        w_f = _pad_cols(w_i[:, off_f:off_r], LANES)
        w_r = _repack_rwkv_cols(w_i[:, off_r:off_g])
        w_g = w_i[:, off_g:]
        b_f = _pad_cols(row2(fox_b_f[i]), LANES)

        x, h = _ffn(i, x, row2(ffn1_norm[i]), row2(mix_norm[i]), *ffn1_w, tm=tm, tf=512)

        qkv, qkv_bf = _proj(h, w_qkv, qkv_scale, mode="qkv", tm=tm_wide, tn=512)
        logf = _proj(h, w_f, b_f, mode="logf", tm=tm_wide, tn=LANES)[0][:, :H_FOX]
        proj_r = _proj(h, w_r, jnp.zeros((1, RP_W), F32), mode="plain", tm=tm_wide, tn=512)[0]
        gates = _proj(h, w_g, jnp.zeros((1, 2 * d_model), F32), mode="sigmoid", tm=tm_wide, tn=1024)[0]

        lf_p = logf[:s_len]
        c_row = _cumsum_rows(lf_p.T)
        o_fox_p = _fox_prompt(qkv_bf, c_row.T, s_len=s_len, tq=tq, tk=tq)
        q_s = qkv[s_len:, :C_FOX].reshape(n_dec, H_FOX, DH_FOX)
        k_s = qkv[s_len:, C_FOX:2 * C_FOX].reshape(n_dec, H_FOX, DH_FOX)
        v_s = qkv[s_len:, 2 * C_FOX:].reshape(n_dec, H_FOX, DH_FOX)
        lf_s = logf[s_len:]
        o_fox_s = _fox_decode(
            i, page_table, q_s, k_s, v_s, jnp.tile(lf_s, (1, LANES // H_FOX)).reshape(n_dec, 1, LANES),
            cache_k, cache_v, cache_lf_t, pps=pps)
        o_fox = jnp.concatenate([o_fox_p, o_fox_s.reshape(n_dec, C_FOX).astype(BF16)], axis=0)

        vecs = (_repack_rwkv_cols(row2(rwkv_mu[i])), row2(rwkv_w0[i]), row2(rwkv_a0[i]),
                row2(rwkv_k_k[i]), row2(rwkv_k_a[i]), row2(rwkv_r_k[i]))
        mats = (_pad_rows(rwkv_w2[i], LANES).astype(BF16), _pad_rows(rwkv_a2[i], LANES).astype(BF16),
                _pad_rows(rwkv_g2[i], 2 * LANES).astype(BF16))
        prev_s = _repack_rwkv_cols(state_shift[i].astype(F32))
        pre = _rwkv_prep(proj_r, prev_s, vecs, mats, sel, s_len=s_len, tm=t_rwkv)
        y_p, wkv_p = _rwkv_scan(*pre[:6], t=s_len)
        y_s, wkv_s = _rwkv_step(i, *[z[s_len:] for z in pre[:6]], state_wkv.astype(F32), nb=8)
        o_rwkv = _rwkv_post(y_p, y_s, pre[6], pre[7], row2(rwkv_ln_g[i]), row2(rwkv_ln_b[i]), sel, tm=t_rwkv)

        x = _merge(i, x, o_fox, o_rwkv, gates, *mix_w, tm=tm, tn=512)
        x, u = _ffn(i, x, row2(ffn2_norm[i]), row2(ple_norm[i]), *ffn2_w, tm=tm, tf=512)
        p_all = jnp.concatenate([p_prompt[i, 0], p_sample[i, :, 0]], axis=0)
        x = _ple(i, x, u, p_all, *ple_w, tm=tm_wide, tn=512)

        outs["kp"].append(qkv[:s_len, C_FOX:2 * C_FOX].reshape(1, s_len, H_FOX, DH_FOX))
        outs["vp"].append(qkv[:s_len, 2 * C_FOX:].reshape(1, s_len, H_FOX, DH_FOX))
        outs["lfp"].append(lf_p.reshape(1, s_len, H_FOX))
        outs["wkvp"].append(wkv_p[None])
        outs["shp"].append(_unpack_rwkv_cols(proj_r[s_len - 1:s_len]))
        outs["ks"].append(k_s.reshape(n_dec, 1, H_FOX, DH_FOX))
        outs["vs"].append(v_s.reshape(n_dec, 1, H_FOX, DH_FOX))
        outs["lfs"].append(lf_s.reshape(n_dec, 1, H_FOX))
        outs["wkvs"].append(wkv_s)
        outs["shs"].append(_unpack_rwkv_cols(proj_r[s_len:]))

    y_p, y_s = _final_norm(x, row2(final_norm), s_len=s_len, tm=t_rwkv)
    st = {k: jnp.stack(v) for k, v in outs.items()}
    return (y_p.reshape(1, s_len, d_model), y_s.reshape(n_dec, 1, d_model),
            st["kp"], st["vp"], st["lfp"], st["wkvp"], st["shp"],
            st["ks"], st["vs"], st["lfs"], st["wkvs"], st["shs"])
```

```python
import functools

import jax
import jax.numpy as jnp
from jax import lax
from jax.experimental import pallas as pl
from jax.experimental.pallas import tpu as pltpu

F32 = jnp.float32
BF16 = jnp.bfloat16

NORM_EPS = 1e-6
LN_X_EPS = 64e-5
KK_EPS = 1e-12

LANES = 128
VMEM_LIMIT = 56 * 1024 * 1024

H_FOX = 8
DH_FOX = 128
C_FOX = H_FOX * DH_FOX
H_RWKV = 16
DH_RWKV = 64
C_RWKV = H_RWKV * DH_RWKV
W_LORA = 64
A_LORA = 64
G_LORA = 160
N_RWKV = 3 * C_RWKV + W_LORA + A_LORA + G_LORA
RP_WL = 3 * C_RWKV
RP_AL = RP_WL + LANES
RP_GL = RP_AL + LANES
RP_W = RP_GL + 2 * LANES
LOG2E = 1.4426950408889634
DECODE_PAGES_PER_STEP = 16
SCAN_CHUNK = 64


def _cparams(sem):
    return pltpu.CompilerParams(dimension_semantics=sem, vmem_limit_bytes=VMEM_LIMIT)


def _sigmoid(x):
    return 1.0 / (1.0 + jnp.exp(-x))


def _softplus(x):
    return jnp.maximum(x, 0.0) + jnp.log(1.0 + jnp.exp(-jnp.abs(x)))


def _rms_rows(x, g):
    ms = jnp.mean(x * x, axis=-1, keepdims=True)
    return x * lax.rsqrt(ms + NORM_EPS) * g


def _dot(a, b):
    return jnp.dot(a, b, preferred_element_type=F32)


def _split2(x):
    hi = x.astype(BF16)
    lo = (x - hi.astype(F32)).astype(BF16)
    return hi, lo


def _ffn_kernel(x_ref, g_ref, gn_ref, wg_ref, wu_ref, wd_ref, o_ref, hn_ref, h_scr, acc_scr):
    f = pl.program_id(1)

    @pl.when(f == 0)
    def _():
        h_scr[...] = _rms_rows(x_ref[...], g_ref[...]).astype(BF16)
        acc_scr[...] = jnp.zeros_like(acc_scr)

    h = h_scr[...]
    gate = _dot(h, wg_ref[...])
    up = _dot(h, wu_ref[...])
    act = (gate * _sigmoid(gate) * up).astype(BF16)
    acc_scr[...] += _dot(act, wd_ref[...])

    @pl.when(f == pl.num_programs(1) - 1)
    def _():
        out = x_ref[...] + 0.5 * acc_scr[...]
        o_ref[...] = out
        hn_ref[...] = _rms_rows(out, gn_ref[...]).astype(BF16)


def _cast_kernel(w_ref, o_ref):
    o_ref[...] = w_ref[...].astype(o_ref.dtype)


def _to_bf16(w):
    depth, rows, cols = w.shape
    tr = next(t for t in (512, 256, 128, 64, 32, 16) if rows % t == 0 and t * cols * 4 <= 6 * 2 ** 20)
    spec = pl.BlockSpec((None, tr, cols), lambda l, r: (l, r, 0))
    return pl.pallas_call(
        _cast_kernel,
        grid=(depth, rows // tr),
        in_specs=[spec],
        out_specs=spec,
        out_shape=jax.ShapeDtypeStruct(w.shape, BF16),
        compiler_params=_cparams(("parallel", "parallel")),
        name="to_bf16",
    )(w)


def _ffn(layer, x, g, g_next, wg, wu, wd, *, tm, tf):
    m, d = x.shape
    dff = wg.shape[2]
    row = pl.BlockSpec((tm, d), lambda i, f: (i, 0))
    vec = pl.BlockSpec((1, d), lambda i, f: (0, 0))
    return pl.pallas_call(
        _ffn_kernel,
        grid=(m // tm, dff // tf),
        in_specs=[
            row, vec, vec,
            pl.BlockSpec((None, d, tf), lambda i, f: (layer, 0, f)),
            pl.BlockSpec((None, d, tf), lambda i, f: (layer, 0, f)),
            pl.BlockSpec((None, tf, d), lambda i, f: (layer, f, 0)),
        ],
        out_specs=[row, row],
        out_shape=[jax.ShapeDtypeStruct((m, d), F32), jax.ShapeDtypeStruct((m, d), BF16)],
        scratch_shapes=[pltpu.VMEM((tm, d), BF16), pltpu.VMEM((tm, d), F32)],
        compiler_params=_cparams(("parallel", "arbitrary")),
        name="ffn",
    )(x, g, g_next, wg, wu, wd)


def _proj_kernel(h_ref, w_ref, aux_ref, *outs, mode):
    acc = _dot(h_ref[...], w_ref[...])
    if mode == "qkv":
        outs[0][...] = acc
        outs[1][...] = (acc * aux_ref[...]).astype(BF16)
    elif mode == "logf":
        outs[0][...] = -_softplus(-(acc + aux_ref[...]))
    else:
        outs[0][...] = acc


def _proj(h, w, aux, *, mode, tm, tn):
    m, d = h.shape
    n = w.shape[1]
    out_shape = [jax.ShapeDtypeStruct((m, n), F32)]
    if mode == "qkv":
        out_shape.append(jax.ShapeDtypeStruct((m, n), BF16))
    o_spec = pl.BlockSpec((tm, tn), lambda i, j: (i, j))
    return pl.pallas_call(
        functools.partial(_proj_kernel, mode=mode),
        grid=(m // tm, n // tn),
        in_specs=[
            pl.BlockSpec((tm, d), lambda i, j: (i, 0)),
            pl.BlockSpec((d, tn), lambda i, j: (0, j)),
            pl.BlockSpec((1, tn), lambda i, j: (0, j)),
        ],
        out_specs=[o_spec] * len(out_shape),
        out_shape=out_shape,
        compiler_params=_cparams(("parallel", "arbitrary")),
        name="proj_" + mode,
    )(h, w, aux)


def _merge_kernel(x_ref, h_ref, of_ref, or_ref, wgf_ref, wgr_ref, wof_ref, wor_ref, wout_ref, o_ref, acc_scr):
    n = pl.program_id(1)

    @pl.when(n == 0)
    def _():
        acc_scr[...] = jnp.zeros_like(acc_scr)

    h = h_ref[...]
    merged = (_sigmoid(_dot(h, wgf_ref[...])) * _dot(of_ref[...], wof_ref[...])
              + _sigmoid(_dot(h, wgr_ref[...])) * _dot(or_ref[...], wor_ref[...]))
    acc_scr[...] += _dot(merged.astype(BF16), wout_ref[...])

    @pl.when(n == pl.num_programs(1) - 1)
    def _():
        o_ref[...] = x_ref[...] + acc_scr[...]


def _merge(layer, x, h, o_fox, o_rwkv, w_gates, wof, wor, wout, *, tm, tn):
    m, d = x.shape
    nb = d // tn
    return pl.pallas_call(
        _merge_kernel,
        grid=(m // tm, nb),
        in_specs=[
            pl.BlockSpec((tm, d), lambda i, n: (i, 0)),
            pl.BlockSpec((tm, d), lambda i, n: (i, 0)),
            pl.BlockSpec((tm, o_fox.shape[1]), lambda i, n: (i, 0)),
            pl.BlockSpec((tm, o_rwkv.shape[1]), lambda i, n: (i, 0)),
            pl.BlockSpec((d, tn), lambda i, n: (0, n)),
            pl.BlockSpec((d, tn), lambda i, n: (0, n + nb)),
            pl.BlockSpec((None, wof.shape[1], tn), lambda i, n: (layer, 0, n)),
            pl.BlockSpec((None, wor.shape[1], tn), lambda i, n: (layer, 0, n)),
            pl.BlockSpec((None, tn, d), lambda i, n: (layer, n, 0)),
        ],
        out_specs=pl.BlockSpec((tm, d), lambda i, n: (i, 0)),
        out_shape=jax.ShapeDtypeStruct((m, d), F32),
        scratch_shapes=[pltpu.VMEM((tm, d), F32)],
        compiler_params=_cparams(("parallel", "arbitrary")),
        name="merge",
    )(x, h, o_fox, o_rwkv, w_gates, w_gates, wof, wor, wout)


def _ple_kernel(x_ref, u_ref, p_ref, wg_ref, wp_ref, o_ref):
    gate = _sigmoid(_dot(u_ref[...], wg_ref[...]))
    emb = _dot(p_ref[...].astype(BF16), wp_ref[...])
    o_ref[...] = x_ref[...] + gate * emb


def _ple(layer, x, u, p, wg, wp, *, tm, tn):
    m, d = x.shape
    return pl.pallas_call(
        _ple_kernel,
        grid=(m // tm, d // tn),
        in_specs=[
            pl.BlockSpec((tm, tn), lambda i, j: (i, j)),
            pl.BlockSpec((tm, d), lambda i, j: (i, 0)),
            pl.BlockSpec((tm, p.shape[1]), lambda i, j: (i, 0)),
            pl.BlockSpec((None, d, tn), lambda i, j: (layer, 0, j)),
            pl.BlockSpec((None, p.shape[1], tn), lambda i, j: (layer, 0, j)),
        ],
        out_specs=pl.BlockSpec((tm, tn), lambda i, j: (i, j)),
        out_shape=jax.ShapeDtypeStruct((m, d), F32),
        compiler_params=_cparams(("parallel", "arbitrary")),
        name="ple",
    )(x, u, p, wg, wp)


def _norm_kernel(x_ref, g_ref, seq_ref, rows_ref, *, seq_tiles):
    y = _rms_rows(x_ref[...], g_ref[...])

    @pl.when(pl.program_id(0) < seq_tiles)
    def _():
        seq_ref[...] = y

    @pl.when(pl.program_id(0) >= seq_tiles)
    def _():
        rows_ref[...] = y


def _final_norm(x, g, *, s_len, tm):
    m, d = x.shape
    assert s_len % tm == 0 and (m - s_len) % tm == 0
    seq_tiles = s_len // tm
    return pl.pallas_call(
        functools.partial(_norm_kernel, seq_tiles=seq_tiles),
        grid=(m // tm,),
        in_specs=[pl.BlockSpec((tm, d), lambda i: (i, 0)), pl.BlockSpec((1, d), lambda i: (0, 0))],
        out_specs=[pl.BlockSpec((tm, d), lambda i: (jnp.minimum(i, seq_tiles - 1), 0)),
                   pl.BlockSpec((tm, d), lambda i: (jnp.maximum(i - seq_tiles, 0), 0))],
        out_shape=[jax.ShapeDtypeStruct((s_len, d), F32), jax.ShapeDtypeStruct((m - s_len, d), F32)],
        compiler_params=_cparams(("arbitrary",)),
        name="final_norm",
    )(x, g)


def _cumsum_kernel(lf_ref, o_ref):
    h, s = lf_ref.shape
    row = lax.broadcasted_iota(jnp.int32, (LANES, LANES), 0)
    col = lax.broadcasted_iota(jnp.int32, (LANES, LANES), 1)
    tri = jnp.where(row <= col, 1.0, 0.0)

    def body(c, carry):
        off = pl.multiple_of(c * LANES, LANES)
        y = jnp.dot(lf_ref[:, pl.ds(off, LANES)], tri, precision=lax.Precision.HIGHEST,
                    preferred_element_type=F32) + carry
        o_ref[:, pl.ds(off, LANES)] = y
        return y[:, LANES - 1:LANES]

    lax.fori_loop(0, s // LANES, body, jnp.zeros((h, 1), F32))


def _cumsum_rows(lf_t):
    return pl.pallas_call(
        _cumsum_kernel,
        out_shape=jax.ShapeDtypeStruct(lf_t.shape, F32),
        compiler_params=pltpu.CompilerParams(vmem_limit_bytes=VMEM_LIMIT),
        name="logf_cumsum",
    )(lf_t)


def _fox_prompt_kernel(q_ref, k_ref, v_ref, qx_ref, kx_ref, o_ref, m_scr, l_scr, acc_scr, *, tq, tk):
    qi = pl.program_id(0)
    kj = pl.program_id(1)

    @pl.when(kj == 0)
    def _():
        m_scr[...] = jnp.full_like(m_scr, -jnp.inf)
        l_scr[...] = jnp.zeros_like(l_scr)
        acc_scr[...] = jnp.zeros_like(acc_scr)

    def update(on_diagonal):
        if on_diagonal:
            causal = (lax.broadcasted_iota(jnp.int32, (tq, tk), 1)
                      <= lax.broadcasted_iota(jnp.int32, (tq, tk), 0))
        heads = range(H_FOX)
        sl = [slice(h * DH_FOX, (h + 1) * DH_FOX) for h in heads]
        ones = jnp.ones((tk, LANES), BF16)
        t = []
        for h in heads:
            q_wide = jnp.concatenate([q_ref[:, sl[h]], qx_ref[h]], axis=1)
            k_wide = jnp.concatenate([k_ref[:, sl[h]], kx_ref[h]], axis=1)
            th = lax.dot_general(q_wide, k_wide, (_NT, ((), ())), preferred_element_type=F32)
            t.append(jnp.where(causal, th, -jnp.inf) if on_diagonal else th)
        m_old = [m_scr[h] for h in heads]
        m_new = [jnp.maximum(m_old[h], jnp.max(t[h], axis=-1, keepdims=True)) for h in heads]
        alpha = [jnp.exp2(m_old[h] - m_new[h]) for h in heads]
        p = [jnp.exp2(t[h] - jnp.tile(m_new[h], (1, tk // LANES))).astype(BF16) for h in heads]
        for h in heads:
            pv = _dot(p[h], jnp.concatenate([v_ref[:, sl[h]], ones], axis=1))
            acc_scr[h] = alpha[h] * acc_scr[h] + pv[:, :DH_FOX]
            l_scr[h] = alpha[h] * l_scr[h] + pv[:, DH_FOX:]
            m_scr[h] = m_new[h]

    @pl.when(kj < qi)
    def _():
        update(False)

    @pl.when(kj == qi)
    def _():
        update(True)
        for h in range(H_FOX):
            o_ref[:, h * DH_FOX:(h + 1) * DH_FOX] = (acc_scr[h] / l_scr[h]).astype(o_ref.dtype)


def _forget_blocks(c_col):
    def top16(x):
        bits = lax.bitcast_convert_type(x, jnp.uint32) & jnp.uint32(0xFFFF0000)
        return lax.bitcast_convert_type(bits, F32)

    c2 = (c_col * LOG2E).T[:, :, None]
    hi = top16(c2)
    mid = top16(c2 - hi)
    lo = c2 - hi - mid
    lane = lax.broadcasted_iota(jnp.int32, c2.shape[:2] + (LANES,), 2)
    part = lambda j: jnp.where(lane == j, hi, jnp.where(lane == j + 1, mid, jnp.where(lane == j + 2, lo, 0.0)))
    one = lambda j: jnp.where((lane >= j) & (lane < j + 3), 1.0, 0.0)
    qx = (part(0) + one(3)).astype(BF16)
    kx = (one(0) - part(3)).astype(BF16)
    return qx, kx


def _fox_prompt(qkv_bf, c_col, *, s_len, tq, tk):
    assert tq == tk and DH_FOX == LANES
    nq, nk = s_len // tq, s_len // tk
    qx, kx = _forget_blocks(c_col)
    return pl.pallas_call(
        functools.partial(_fox_prompt_kernel, tq=tq, tk=tk),
        grid=(nq, nk),
        in_specs=[
            pl.BlockSpec((tq, C_FOX), lambda i, j: (i, 0)),
            pl.BlockSpec((tk, C_FOX), lambda i, j: (jnp.minimum(i, j), 1)),
            pl.BlockSpec((tk, C_FOX), lambda i, j: (jnp.minimum(i, j), 2)),
            pl.BlockSpec((H_FOX, tq, LANES), lambda i, j: (0, i, 0)),
            pl.BlockSpec((H_FOX, tk, LANES), lambda i, j: (0, jnp.minimum(i, j), 0)),
        ],
        out_specs=pl.BlockSpec((tq, C_FOX), lambda i, j: (i, 0)),
        out_shape=jax.ShapeDtypeStruct((s_len, C_FOX), BF16),
        scratch_shapes=[
            pltpu.VMEM((H_FOX, tq, LANES), F32),
            pltpu.VMEM((H_FOX, tq, LANES), F32),
            pltpu.VMEM((H_FOX, tq, DH_FOX), F32),
        ],
        compiler_params=_cparams(("parallel", "arbitrary")),
        name="fox_prompt",
    )(qkv_bf, qkv_bf, qkv_bf, qx, kx)


def _fox_decode_kernel(pt_ref, q_ref, kn_ref, vn_ref, lfn_ref, sfx_ref, *rest, scale, pps):
    del pt_ref
    kc, vc, lfc = rest[:pps], rest[pps:2 * pps], rest[2 * pps:3 * pps]
    o_ref, m_scr, l_scr, acc_scr, carry_scr = rest[3 * pps:]
    p = pl.program_id(1)
    page = kc[0].shape[0]
    nrow = page * H_FOX
    sub = lax.broadcasted_iota(jnp.int32, (H_FOX, nrow + LANES), 0)
    lane = lax.broadcasted_iota(jnp.int32, (H_FOX, nrow + LANES), 1)
    own_ext = (lane % H_FOX) == sub
    own = own_ext[:, :nrow]
    pick = lane[:, :LANES] == sub[:, :LANES]
    q8 = q_ref[...] * scale

    def to_col(row):
        return jnp.sum(jnp.where(pick, row, 0.0), axis=-1, keepdims=True)

    def per_head(row, op):
        parts = [row[:, j * LANES:(j + 1) * LANES] for j in range(nrow // LANES)]
        while len(parts) > 1:
            parts = [op(parts[i], parts[i + 1]) for i in range(0, len(parts), 2)]
        acc = parts[0]
        shift = H_FOX
        while shift < LANES:
            acc = op(acc, pltpu.roll(acc, shift, axis=1))
            shift *= 2
        return acc

    @pl.when(p == 0)
    def _():
        m_scr[...] = jnp.full_like(m_scr, -jnp.inf)
        l_scr[...] = jnp.zeros_like(l_scr)
        acc_scr[...] = jnp.zeros_like(acc_scr)
        carry_scr[...] = lfn_ref[...]

    q8_bf = q8.astype(BF16)
    widen = lambda row: jnp.tile(row, (1, nrow // LANES))
    carry = carry_scr[...]
    scores = []
    for j in range(pps):
        k2 = kc[j][...].reshape(nrow, DH_FOX).astype(BF16)
        g = lax.dot_general(q8_bf, k2, (_NT, ((), ())), preferred_element_type=F32)
        s = jnp.sum(jnp.where(own, g, 0.0), axis=0, keepdims=True)

        lf = lfc[j][...]
        hi = lf.astype(BF16).astype(F32)
        lf2 = jnp.concatenate([hi, lf - hi], axis=1).astype(BF16)
        t = _dot(lf2, sfx_ref[...])
        bias = jnp.sum(jnp.where(own_ext, t, 0.0), axis=0, keepdims=True)
        scores.append(s + bias[:, :nrow] + widen(carry))
        carry = carry + bias[:, nrow:]
    carry_scr[...] = carry

    m_old = m_scr[...]
    m_new = m_old
    for s in scores:
        m_new = jnp.maximum(m_new, per_head(s, jnp.maximum))
    alpha = jnp.exp(m_old - m_new)
    m_wide = widen(m_new)
    l_new = alpha * l_scr[...]
    acc = to_col(alpha) * acc_scr[...]
    for j, s in enumerate(scores):
        pr = jnp.exp(s - m_wide)
        l_new = l_new + per_head(pr, jnp.add)
        p8 = jnp.where(own, pr, 0.0).astype(BF16)
        acc = acc + _dot(p8, vc[j][...].reshape(nrow, DH_FOX).astype(BF16))
    l_scr[...] = l_new
    acc_scr[...] = acc
    m_scr[...] = m_new

    @pl.when(p == pl.num_programs(1) - 1)
    def _():
        s_new = jnp.sum(q8 * kn_ref[...], axis=-1, keepdims=True)
        m_old = to_col(m_scr[...])
        m_fin = jnp.maximum(m_old, s_new)
        alpha = jnp.exp(m_old - m_fin)
        p_new = jnp.exp(s_new - m_fin)
        l_fin = alpha * to_col(l_scr[...]) + p_new
        o_ref[...] = (alpha * acc_scr[...] + p_new * vn_ref[...]) / l_fin


def _suffix_selector(page):
    r = jnp.arange(page * H_FOX + LANES)
    tok = jnp.arange(page)
    later = jnp.where(r[None, :] < page * H_FOX, tok[:, None] > r[None, :] // H_FOX, True)
    return jnp.tile(later.astype(BF16), (2, 1))


def _fox_decode(layer, page_table, q, k_new, v_new, lf_new, cache_k, cache_v, cache_lf, *, pps):
    n, n_pages = page_table.shape
    page = cache_k.shape[2]
    pt_flat = page_table.reshape(-1)
    sfx = _suffix_selector(page)

    def page_map(j, tail):
        def index_map(b, p, pt):
            return (layer, pt[b * n_pages + (n_pages - 1 - (p * pps + j))]) + tail
        return index_map

    row_spec = pl.BlockSpec((None, H_FOX, DH_FOX), lambda b, p, pt: (b, 0, 0))
    kv_specs = [pl.BlockSpec((None, None, page, H_FOX, DH_FOX), page_map(j, (0, 0, 0))) for j in range(pps)]
    lf_specs = [pl.BlockSpec((None, None, H_FOX, page), page_map(j, (0, 0))) for j in range(pps)]
    grid_spec = pltpu.PrefetchScalarGridSpec(
        num_scalar_prefetch=1,
        grid=(n, n_pages // pps),
        in_specs=[
            row_spec, row_spec, row_spec,
            pl.BlockSpec((None, 1, LANES), lambda b, p, pt: (b, 0, 0)),
            pl.BlockSpec(sfx.shape, lambda b, p, pt: (0, 0)),
        ] + kv_specs + kv_specs + lf_specs,
        out_specs=row_spec,
        scratch_shapes=[
            pltpu.VMEM((1, LANES), F32),
            pltpu.VMEM((1, LANES), F32),
            pltpu.VMEM((H_FOX, DH_FOX), F32),
            pltpu.VMEM((1, LANES), F32),
        ],
    )
    return pl.pallas_call(
        functools.partial(_fox_decode_kernel, scale=DH_FOX ** -0.5, pps=pps),
        grid_spec=grid_spec,
        out_shape=jax.ShapeDtypeStruct((n, H_FOX, DH_FOX), F32),
        compiler_params=_cparams(("parallel", "arbitrary")),
        name="fox_decode",
    )(pt_flat, q, k_new, v_new, lf_new, sfx, *([cache_k] * pps), *([cache_v] * pps), *([cache_lf] * pps))


def _seg_sum(x, e_ref, et_ref):
    hi, lo = _split2(x)
    s = _dot(hi, e_ref[...]) + _dot(lo, e_ref[...])
    shi, slo = _split2(s)
    return _dot(shi, et_ref[...]) + _dot(slo, et_ref[...])


def _rwkv_prep_kernel(pr_ref, pv_ref, ps_ref, mu_ref, w0_ref, a0_ref, kk_ref, ka_ref, rk_ref,
                      w2_ref, a2_ref, g2_ref, e_ref, et_ref,
                      r_out, lw_out, k_out, v_out, a_out, b_out, g_out, bonus_out, *, seq_tiles):
    i = pl.program_id(0)
    pr = pr_ref[...]
    first = jnp.where(i == 0, 0.0, pv_ref[7:8, :])
    prev = jnp.where(lax.broadcasted_iota(jnp.int32, pr.shape, 0) == 0, first, pltpu.roll(pr, 1, axis=0))
    prev = jnp.where(i < seq_tiles, prev, ps_ref[...])
    xm = pr + (prev - pr) * mu_ref[...]
    r = xm[:, 0:C_RWKV]
    k = xm[:, C_RWKV:2 * C_RWKV]
    v = xm[:, 2 * C_RWKV:3 * C_RWKV]
    wl = xm[:, RP_WL:RP_AL]
    al = xm[:, RP_AL:RP_GL]
    gl = xm[:, RP_GL:RP_W]

    w_log = -_softplus(-(w0_ref[...] + _dot(jnp.tanh(wl).astype(BF16), w2_ref[...]))) - 0.5
    a_sig = _sigmoid(a0_ref[...] + _dot(al.astype(BF16), a2_ref[...]))
    g = _dot(_sigmoid(gl).astype(BF16), g2_ref[...])

    kk = k * kk_ref[...]
    norm = jnp.sqrt(_seg_sum(kk * kk, e_ref, et_ref))
    kk = kk / jnp.maximum(norm, KK_EPS)
    k_mod = k * (1.0 + (a_sig - 1.0) * ka_ref[...])

    r_out[...] = r
    lw_out[...] = -jnp.exp(w_log)
    k_out[...] = k_mod
    v_out[...] = v
    a_out[...] = -kk
    b_out[...] = kk * a_sig
    g_out[...] = g
    bonus_out[...] = _seg_sum(r * k_mod * rk_ref[...], e_ref, et_ref) * v


def _rwkv_prep(proj, prev_rows, vecs, mats, sel, *, s_len, tm):
    m = proj.shape[0]
    assert s_len % tm == 0 and (m - s_len) % tm == 0
    seq_tiles = s_len // tm
    mu, w0, a0, k_k, k_a, r_k = vecs
    w2, a2, g2 = mats
    e, et = sel
    row = lambda i: (i, 0)
    fix = lambda i: (0, 0)
    vec_spec = pl.BlockSpec((1, C_RWKV), fix)
    out = jax.ShapeDtypeStruct((m, C_RWKV), F32)
    return pl.pallas_call(
        functools.partial(_rwkv_prep_kernel, seq_tiles=seq_tiles),
        grid=(m // tm,),
        in_specs=[
            pl.BlockSpec((tm, RP_W), row),
            pl.BlockSpec((8, RP_W), lambda i: (jnp.maximum(i * (tm // 8) - 1, 0), 0)),
            pl.BlockSpec((tm, RP_W), lambda i: (jnp.maximum(i - seq_tiles, 0), 0)),
            pl.BlockSpec((1, RP_W), fix),
            vec_spec, vec_spec, vec_spec, vec_spec, vec_spec,
            pl.BlockSpec(w2.shape, fix), pl.BlockSpec(a2.shape, fix), pl.BlockSpec(g2.shape, fix),
            pl.BlockSpec(e.shape, fix), pl.BlockSpec(et.shape, fix),
        ],
        out_specs=[pl.BlockSpec((tm, C_RWKV), row)] * 8,
        out_shape=[out] * 8,
        compiler_params=_cparams(("parallel",)),
        name="rwkv_prep",
    )(proj, proj, prev_rows, mu, w0, a0, k_k, k_a, r_k, w2, a2, g2, e, et)


_NN = ((1,), (0,))
_NT = ((1,), (1,))
_TN = ((0,), (0,))


def _mm(a, b, dims):
    return lax.dot_general(a, b, (dims, ((), ())), preferred_element_type=F32)


def _rwkv_scan_kernel(r_ref, lw_ref, k_ref, v_ref, a_ref, b_ref, y_ref, s_out, s_scr):
    c = pl.program_id(0)
    ch = r_ref.shape[0]
    heads = range(H_RWKV)

    @pl.when(c == 0)
    def _():
        s_scr[...] = jnp.zeros_like(s_scr)

    row = lax.broadcasted_iota(jnp.int32, (ch, ch), 0)
    col = lax.broadcasted_iota(jnp.int32, (ch, ch), 1)
    incl = row >= col
    strict = row > col

    def cut(x):
        return [x[:, h * DH_RWKV:(h + 1) * DH_RWKV].astype(BF16) for h in heads]

    lw = lw_ref[...]
    cs = jnp.dot(jnp.where(incl, 1.0, 0.0), lw, precision=lax.Precision.HIGHEST, preferred_element_type=F32)
    p_incl = jnp.exp(cs)
    p_inv = jnp.exp(-cs)
    p_excl = jnp.exp(cs - lw)
    rm = cut(r_ref[...] * p_incl)
    am = cut(a_ref[...] * p_excl)
    bm = cut(b_ref[...] * p_inv)
    km = cut(k_ref[...] * p_inv)
    vm = cut(v_ref[...])
    p_end = p_incl[ch - 1:ch, :]
    s0 = [s_scr[h] for h in heads]
    s0b = [s.astype(BF16) for s in s0]

    def masked(xs, ys, keep):
        return [jnp.where(keep, _mm(x, y, _NT), 0.0).astype(BF16) for x, y in zip(xs, ys)]

    l_ab = masked(am, bm, strict)
    l_ak = masked(am, km, strict)
    m_rb = masked(rm, bm, incl)
    m_rk = masked(rm, km, incl)
    u = [_mm(am[h], s0b[h], _NT) + _mm(l_ak[h], vm[h], _NN) for h in heads]
    y = [_mm(rm[h], s0b[h], _NT) + _mm(m_rk[h], vm[h], _NN) for h in heads]
    pw = l_ab
    u = [u[h] + _mm(pw[h], u[h].astype(BF16), _NN) for h in heads]
    span = 2
    while span < ch:
        pw = [_mm(pw[h], pw[h], _NN).astype(BF16) for h in heads]
        u = [u[h] + _mm(pw[h], u[h].astype(BF16), _NN) for h in heads]
        span *= 2
    ub = [x.astype(BF16) for x in u]
    y = [y[h] + _mm(m_rb[h], ub[h], _NN) for h in heads]
    y_ref[...] = jnp.concatenate(y, axis=1)
    for h in heads:
        s_new = s0[h] + _mm(ub[h], bm[h], _TN) + _mm(vm[h], km[h], _TN)
        s_scr[h] = s_new * p_end[:, h * DH_RWKV:(h + 1) * DH_RWKV]

    @pl.when(c == pl.num_programs(0) - 1)
    def _():
        s_out[...] = s_scr[...]


def _rwkv_scan(r, lw, k, v, a, b, *, t):
    spec = pl.BlockSpec((SCAN_CHUNK, C_RWKV), lambda c: (c, 0))
    st = (H_RWKV, DH_RWKV, DH_RWKV)
    return pl.pallas_call(
        _rwkv_scan_kernel,
        grid=(t // SCAN_CHUNK,),
        in_specs=[spec] * 6,
        out_specs=[spec, pl.BlockSpec(st, lambda c: (0, 0, 0))],
        out_shape=[jax.ShapeDtypeStruct((t, C_RWKV), F32), jax.ShapeDtypeStruct(st, F32)],
        scratch_shapes=[pltpu.VMEM(st, F32)],
        compiler_params=_cparams(("arbitrary",)),
        name="rwkv_scan",
    )(r, lw, k, v, a, b)


def _rwkv_step_kernel(r_ref, lw_ref, k_ref, v_ref, a_ref, b_ref, s_ref, y_ref, s_out):
    s = s_ref[...]
    dv, dk = s.shape[-2:]
    eye = (lax.broadcasted_iota(jnp.int32, (dv, dk), 0) == lax.broadcasted_iota(jnp.int32, (dv, dk), 1))
    sa = jnp.sum(s * a_ref[...], axis=-1, keepdims=True)
    v_col = jnp.sum(jnp.where(eye, v_ref[...], 0.0), axis=-1, keepdims=True)
    s_new = s * jnp.exp(lw_ref[...]) + sa * b_ref[...] + v_col * k_ref[...]
    y_col = jnp.sum(s_new * r_ref[...], axis=-1, keepdims=True)
    y_ref[...] = jnp.sum(jnp.where(eye, y_col, 0.0), axis=-2, keepdims=True)
    s_out[...] = s_new


def _rwkv_step(layer, r, lw, k, v, a, b, state, *, nb):
    n = state.shape[1]
    vec = lambda z: z.reshape(n, H_RWKV, 1, DH_RWKV)
    vspec = pl.BlockSpec((nb, H_RWKV, 1, DH_RWKV), lambda i: (i, 0, 0, 0))
    sspec = pl.BlockSpec((nb, H_RWKV, DH_RWKV, DH_RWKV), lambda i: (i, 0, 0, 0))
    sspec_in = pl.BlockSpec((None, nb, H_RWKV, DH_RWKV, DH_RWKV), lambda i: (layer, i, 0, 0, 0))
    y, s_new = pl.pallas_call(
        _rwkv_step_kernel,
        grid=(n // nb,),
        in_specs=[vspec] * 6 + [sspec_in],
        out_specs=[vspec, sspec],
        out_shape=[jax.ShapeDtypeStruct((n, H_RWKV, 1, DH_RWKV), F32),
                   jax.ShapeDtypeStruct(state.shape[1:], F32)],
        compiler_params=_cparams(("parallel",)),
        name="rwkv_step",
    )(vec(r), vec(lw), vec(k), vec(v), vec(a), vec(b), state)
    return y.reshape(n, C_RWKV), s_new


def _rwkv_post_kernel(yp_ref, ys_ref, g_ref, bonus_ref, lng_ref, lnb_ref, e_ref, et_ref, o_ref, *, seq_tiles):
    y = jnp.where(pl.program_id(0) < seq_tiles, yp_ref[...], ys_ref[...])
    inv_n = 1.0 / DH_RWKV
    mean = _seg_sum(y, e_ref, et_ref) * inv_n
    yc = y - mean
    var = _seg_sum(yc * yc, e_ref, et_ref) * inv_n
    yn = yc * lax.rsqrt(var + LN_X_EPS) * lng_ref[...] + lnb_ref[...]
    o_ref[...] = ((yn + bonus_ref[...]) * g_ref[...]).astype(o_ref.dtype)


def _rwkv_post(y_seq, y_rows, g, bonus, ln_g, ln_b, sel, *, tm):
    m = g.shape[0]
    s_len = y_seq.shape[0]
    assert s_len % tm == 0 and y_rows.shape[0] % tm == 0
    seq_tiles = s_len // tm
    e, et = sel
    row = pl.BlockSpec((tm, C_RWKV), lambda i: (i, 0))
    vec = pl.BlockSpec((1, C_RWKV), lambda i: (0, 0))
    return pl.pallas_call(
        functools.partial(_rwkv_post_kernel, seq_tiles=seq_tiles),
        grid=(m // tm,),
        in_specs=[pl.BlockSpec((tm, C_RWKV), lambda i: (jnp.minimum(i, seq_tiles - 1), 0)),
                  pl.BlockSpec((tm, C_RWKV), lambda i: (jnp.maximum(i - seq_tiles, 0), 0)),
                  row, row, vec, vec,
                  pl.BlockSpec(e.shape, lambda i: (0, 0)), pl.BlockSpec(et.shape, lambda i: (0, 0))],
        out_specs=row,
        out_shape=jax.ShapeDtypeStruct((m, C_RWKV), BF16),
        compiler_params=_cparams(("parallel",)),
        name="rwkv_post",
    )(y_seq, y_rows, g, bonus, ln_g, ln_b, e, et)


def _pad_cols(w, width):
    return jnp.pad(w, ((0, 0), (0, width - w.shape[1])))


def _pad_rows(w, height):
    return jnp.pad(w, ((0, height - w.shape[0]), (0, 0)))


def _repack_rwkv_cols(w):
    wl0 = 3 * C_RWKV
    al0 = wl0 + W_LORA
    gl0 = al0 + A_LORA
    return jnp.concatenate([
        w[:, :wl0],
        _pad_cols(w[:, wl0:al0], LANES),
        _pad_cols(w[:, al0:gl0], LANES),
        _pad_cols(w[:, gl0:], 2 * LANES),
    ], axis=1)


def _unpack_rwkv_cols(w):
    return jnp.concatenate([
        w[:, :RP_WL], w[:, RP_WL:RP_WL + W_LORA], w[:, RP_AL:RP_AL + A_LORA], w[:, RP_GL:RP_GL + G_LORA],
    ], axis=1)


def _head_selectors():
    lane_head = jnp.arange(C_RWKV) // DH_RWKV
    e = (lane_head[:, None] == jnp.arange(LANES)[None, :]).astype(BF16)
    return e, e.T


def kernel(x_prompt, x_sample, cache_k, cache_v, cache_logf, state_wkv, state_shift, page_table,
           p_prompt, p_sample,
           ffn1_norm, ffn1_w_gate, ffn1_w_up, ffn1_w_down, mix_norm, w_in, fox_b_f,
           rwkv_mu, rwkv_w0, rwkv_w2, rwkv_a0, rwkv_a2, rwkv_g2, rwkv_k_k, rwkv_k_a, rwkv_r_k,
           rwkv_ln_g, rwkv_ln_b, w_o_fox, w_o_rwkv, w_out,
           ffn2_norm, ffn2_w_gate, ffn2_w_up, ffn2_w_down, ple_norm, ple_w_gate, ple_w_proj,
           final_norm):
    depth = w_in.shape[0]
    n_pr, s_len, d_model = x_prompt.shape
    n_dec = x_sample.shape[0]
    assert n_pr == 1 and x_sample.shape[1] == 1
    m_rows = s_len + n_dec
    n_pages = page_table.shape[1]
    pps = DECODE_PAGES_PER_STEP if n_pages % DECODE_PAGES_PER_STEP == 0 else 1
    tm = 640 if m_rows % 640 == 0 else LANES
    tm_wide = 1664 if m_rows % 1664 == 0 else tm
    tq = min(512, s_len)
    t_rwkv = n_dec

    off_f = 3 * C_FOX
    off_r = off_f + H_FOX
    off_g = off_r + N_RWKV
    qkv_scale = jnp.concatenate([jnp.full((1, C_FOX), LOG2E * DH_FOX ** -0.5, F32), jnp.ones((1, 2 * C_FOX), F32)], axis=1)
    sel = _head_selectors()
    row2 = lambda z: z.reshape(1, -1).astype(F32)

    x = jnp.concatenate([x_prompt[0], x_sample[:, 0]], axis=0)
    outs = {name: [] for name in ("kp", "vp", "lfp", "wkvp", "shp", "ks", "vs", "lfs", "wkvs", "shs")}
    ffn1_w = [_to_bf16(w) for w in (ffn1_w_gate, ffn1_w_up, ffn1_w_down)]
    ffn2_w = [_to_bf16(w) for w in (ffn2_w_gate, ffn2_w_up, ffn2_w_down)]
    mix_w = [_to_bf16(w) for w in (w_o_fox, w_o_rwkv, w_out)]
    ple_w = [_to_bf16(w) for w in (ple_w_gate, ple_w_proj)]
    w_in_bf = _to_bf16(w_in)
    cache_lf_t = jnp.swapaxes(cache_logf, 2, 3)
    for i in range(depth):
        w_i = w_in_bf[i]
        w_qkv = w_i[:, :off_f]
        w_f = _pad_cols(w_i[:, off_f:off_r], LANES)
        w_r = _repack_rwkv_cols(w_i[:, off_r:off_g])
        w_g = w_i[:, off_g:]
        b_f = _pad_cols(row2(fox_b_f[i]), LANES)

        x, h = _ffn(i, x, row2(ffn1_norm[i]), row2(mix_norm[i]), *ffn1_w, tm=tm, tf=512)

        qkv, qkv_bf = _proj(h, w_qkv, qkv_scale, mode="qkv", tm=tm_wide, tn=512)
        logf = _proj(h, w_f, b_f, mode="logf", tm=tm_wide, tn=LANES)[0][:, :H_FOX]
        proj_r = _proj(h, w_r, jnp.zeros((1, RP_W), F32), mode="plain", tm=tm_wide, tn=512)[0]

        lf_p = logf[:s_len]
        c_row = _cumsum_rows(lf_p.T)
        o_fox_p = _fox_prompt(qkv_bf, c_row.T, s_len=s_len, tq=tq, tk=tq)
        q_s = qkv[s_len:, :C_FOX].reshape(n_dec, H_FOX, DH_FOX)
        k_s = qkv[s_len:, C_FOX:2 * C_FOX].reshape(n_dec, H_FOX, DH_FOX)
        v_s = qkv[s_len:, 2 * C_FOX:].reshape(n_dec, H_FOX, DH_FOX)
        lf_s = logf[s_len:]
        o_fox_s = _fox_decode(
            i, page_table, q_s, k_s, v_s, jnp.tile(lf_s, (1, LANES // H_FOX)).reshape(n_dec, 1, LANES),
            cache_k, cache_v, cache_lf_t, pps=pps)
        o_fox = jnp.concatenate([o_fox_p, o_fox_s.reshape(n_dec, C_FOX).astype(BF16)], axis=0)

        vecs = (_repack_rwkv_cols(row2(rwkv_mu[i])), row2(rwkv_w0[i]), row2(rwkv_a0[i]),
                row2(rwkv_k_k[i]), row2(rwkv_k_a[i]), row2(rwkv_r_k[i]))
        mats = (_pad_rows(rwkv_w2[i], LANES).astype(BF16), _pad_rows(rwkv_a2[i], LANES).astype(BF16),
                _pad_rows(rwkv_g2[i], 2 * LANES).astype(BF16))
        prev_s = _repack_rwkv_cols(state_shift[i].astype(F32))
        pre = _rwkv_prep(proj_r, prev_s, vecs, mats, sel, s_len=s_len, tm=t_rwkv)
        y_p, wkv_p = _rwkv_scan(*pre[:6], t=s_len)
        y_s, wkv_s = _rwkv_step(i, *[z[s_len:] for z in pre[:6]], state_wkv.astype(F32), nb=8)
        o_rwkv = _rwkv_post(y_p, y_s, pre[6], pre[7], row2(rwkv_ln_g[i]), row2(rwkv_ln_b[i]), sel, tm=t_rwkv)

        x = _merge(i, x, h, o_fox, o_rwkv, w_g, *mix_w, tm=tm, tn=512)
        x, u = _ffn(i, x, row2(ffn2_norm[i]), row2(ple_norm[i]), *ffn2_w, tm=tm, tf=512)
        p_all = jnp.concatenate([p_prompt[i, 0], p_sample[i, :, 0]], axis=0)
        x = _ple(i, x, u, p_all, *ple_w, tm=tm_wide, tn=512)

        outs["kp"].append(qkv[:s_len, C_FOX:2 * C_FOX].reshape(1, s_len, H_FOX, DH_FOX))
        outs["vp"].append(qkv[:s_len, 2 * C_FOX:].reshape(1, s_len, H_FOX, DH_FOX))
        outs["lfp"].append(lf_p.reshape(1, s_len, H_FOX))
        outs["wkvp"].append(wkv_p[None])
        outs["shp"].append(_unpack_rwkv_cols(proj_r[s_len - 1:s_len]))
        outs["ks"].append(k_s.reshape(n_dec, 1, H_FOX, DH_FOX))
        outs["vs"].append(v_s.reshape(n_dec, 1, H_FOX, DH_FOX))
        outs["lfs"].append(lf_s.reshape(n_dec, 1, H_FOX))
        outs["wkvs"].append(wkv_s)
        outs["shs"].append(_unpack_rwkv_cols(proj_r[s_len:]))

    y_p, y_s = _final_norm(x, row2(final_norm), s_len=s_len, tm=t_rwkv)
    st = {k: jnp.stack(v) for k, v in outs.items()}
    return (y_p.reshape(1, s_len, d_model), y_s.reshape(n_dec, 1, d_model),
            st["kp"], st["vp"], st["lfp"], st["wkvp"], st["shp"],
            st["ks"], st["vs"], st["lfs"], st["wkvs"], st["shs"])
```
